```python
import math
import jax, jax.numpy as jnp
from jax import lax
import numpy as np

D_MODEL = 1024
BATCH = 8
SEQ = 2048
DEPTH = 4
DEC_BATCH = 32
DEC_SEQ = 4
PAST_LEN = 8192
PAGE_SIZE = 128

H_A = 4
DH_A = 64
DV_A = 2 * DH_A
H_B = 8
DH_B = 64
ROPE_DIM = DH_A // 4
ROPE_THETA = 500000.0
Q_BLOCK = 128
PEER_HEADS = 8
N_KEYS = 128
N_EXPERTS = N_KEYS * N_KEYS
D_KEY = 256
PEER_TOPK = 16
PEER_BLOCK = 128
DEEPNORM_ALPHA = (2.0 * DEPTH) ** 0.25
DEEPNORM_BETA = (8.0 * DEPTH) ** -0.25
LN_EPS = 1e-5
NEG_INF = -1e30
W_A = H_A * DV_A
W_B = H_B * DH_B
COL_SIZES = (H_A * 2 * DH_A, H_A * 2 * DH_A, H_A * DV_A, H_B * DH_B, H_B * DH_B, H_B * DH_B, H_B, 2 * D_MODEL)
IN_COLS = 2 * H_A * 2 * DH_A + H_A * DV_A + 3 * H_B * DH_B + H_B + 2 * D_MODEL

kernel_name = 'hybrid_diff_fox_peer_decoder_step'


def split_cols(proj):
    parts = []
    off = 0
    for size in COL_SIZES:
        parts.append(proj[..., off:off + size])
        off += size
    return parts


def layer_norm(x, g, b):
    xf = x.astype(jnp.float32)
    mu = jnp.mean(xf, axis=-1, keepdims=True)
    var = jnp.mean(jnp.square(xf - mu), axis=-1, keepdims=True)
    return ((xf - mu) * lax.rsqrt(var + LN_EPS) * g + b).astype(x.dtype)


def rms_norm(x, g):
    xf = x.astype(jnp.float32)
    y = xf * lax.rsqrt(jnp.mean(jnp.square(xf), axis=-1, keepdims=True) + LN_EPS)
    return (y * g).astype(x.dtype)


def rope_partial(x, pos):
    half = ROPE_DIM // 2
    inv_freq = ROPE_THETA ** (-jnp.arange(0, ROPE_DIM, 2, dtype=jnp.float32) / ROPE_DIM)
    ang = pos.astype(jnp.float32)[:, None] * inv_freq[None, :]
    shape = (ang.shape[0],) + (1,) * (x.ndim - 3) + (half,)
    cos = jnp.cos(ang).reshape(shape).astype(x.dtype)
    sin = jnp.sin(ang).reshape(shape).astype(x.dtype)
    x1 = x[..., :half]
    x2 = x[..., half:ROPE_DIM]
    return jnp.concatenate([x1 * cos - x2 * sin, x2 * cos + x1 * sin, x[..., ROPE_DIM:]], axis=-1)


def diff_attend(q, qpos, k, v, kpos, lam):
    s = jnp.einsum('bqhcd,bkhcd->bchqk', q, k).astype(jnp.float32) * (DH_A ** -0.5)
    mask = kpos[None, :] <= qpos[:, None]
    p = jax.nn.softmax(jnp.where(mask, s, NEG_INF), axis=-1)
    w = p[:, 0] - lam * p[:, 1]
    return jnp.einsum('bhqk,bkhd->bqhd', w.astype(v.dtype), v)


def fox_attend(q, cq, qpos, k, v, ck, kpos):
    s = jnp.einsum('bqhd,bkhd->bhqk', q, k).astype(jnp.float32) * (DH_B ** -0.5)
    s = s + jnp.moveaxis(cq, 1, 2)[..., :, None] - jnp.moveaxis(ck, 1, 2)[..., None, :]
    mask = kpos[None, :] <= qpos[:, None]
    p = jax.nn.softmax(jnp.where(mask, s, NEG_INF), axis=-1)
    return jnp.einsum('bhqk,bkhd->bqhd', p.astype(v.dtype), v)


def attend_queries(fn, q_args, qpos):
    T = qpos.shape[0]
    if T > Q_BLOCK and T % Q_BLOCK == 0:
        nblk = T // Q_BLOCK

        def to_blocks(a):
            return jnp.moveaxis(a.reshape((a.shape[0], nblk, Q_BLOCK) + a.shape[2:]), 1, 0)

        blocks = tuple(to_blocks(a) for a in q_args) + (qpos.reshape(nblk, Q_BLOCK),)
        out = jnp.moveaxis(lax.map(lambda b: fn(*b), blocks), 0, 1)
        return out.reshape((out.shape[0], T) + out.shape[3:])
    return fn(*q_args, qpos)


def token_mixer(x, pos, layer, w_in_l, b_forget_l, b_gate_l, lam_params, subln_g, w_ba, w_bb, w_o, past):
    B, T, _ = x.shape
    proj = jnp.einsum('btd,dc->btc', x, w_in_l)
    dq, dk, dv, fq, fk, fv, fl, gl = split_cols(proj)
    dq = rope_partial(dq.reshape(B, T, H_A, 2, DH_A), pos)
    dk = rope_partial(dk.reshape(B, T, H_A, 2, DH_A), pos)
    dv = dv.reshape(B, T, H_A, DV_A)
    fq = fq.reshape(B, T, H_B, DH_B)
    fk = fk.reshape(B, T, H_B, DH_B)
    fv = fv.reshape(B, T, H_B, DH_B)
    logf = jax.nn.log_sigmoid((fl + b_forget_l).astype(jnp.float32))
    new_rows = (dk.reshape(B, T, H_A, 2 * DH_A), dv, fk, fv, logf)
    if past is None:
        kd, vd, kf, vf, logf_all = dk, dv, fk, fv, logf
    else:
        pdk, pdv, pfk, pfv, plogf = past
        P = pdk.shape[1]
        kd = jnp.concatenate([pdk.reshape(B, P, H_A, 2, DH_A).astype(dk.dtype), dk], axis=1)
        vd = jnp.concatenate([pdv.astype(dv.dtype), dv], axis=1)
        kf = jnp.concatenate([pfk.astype(fk.dtype), fk], axis=1)
        vf = jnp.concatenate([pfv.astype(fv.dtype), fv], axis=1)
        logf_all = jnp.concatenate([plogf.astype(jnp.float32), logf], axis=1)
    kpos = jnp.arange(kd.shape[1], dtype=jnp.int32)
    cum = jnp.cumsum(logf_all, axis=1)
    cq = cum[:, -T:]

    lam_init = 0.8 - 0.6 * math.exp(-0.3 * layer)
    lp = lam_params.astype(jnp.float32)
    lam = jnp.exp(jnp.sum(lp[0] * lp[1])) - jnp.exp(jnp.sum(lp[2] * lp[3])) + lam_init
    a = attend_queries(lambda q, qp: diff_attend(q, qp, kd, vd, kpos, lam), (dq,), pos)
    a = rms_norm(a, subln_g) * (1.0 - lam_init)
    f = attend_queries(lambda q, c, qp: fox_attend(q, c, qp, kf, vf, cum, kpos), (fq, cq), pos)

    branch_a = jnp.einsum('bte,ed->btd', a.reshape(B, T, W_A), w_ba)
    branch_b = jnp.einsum('bte,ed->btd', f.reshape(B, T, W_B), w_bb)
    gates = jax.nn.sigmoid((gl + b_gate_l).astype(jnp.float32)).astype(x.dtype)
    merged = gates[..., :D_MODEL] * branch_a + gates[..., D_MODEL:] * branch_b
    return jnp.einsum('btd,de->bte', merged, w_o), new_rows


def peer_block(xb, wq, q_g, subkeys, u, v):
    n = xb.shape[0]
    q = rms_norm(jnp.einsum('nd,de->ne', xb, wq).reshape(n, PEER_HEADS, D_KEY), q_g)
    half = D_KEY // 2
    s1 = jnp.einsum('nhd,hkd->nhk', q[..., :half], subkeys[:, 0]).astype(jnp.float32)
    s2 = jnp.einsum('nhd,hkd->nhk', q[..., half:], subkeys[:, 1]).astype(jnp.float32)
    t1, i1 = lax.top_k(s1, PEER_TOPK)
    t2, i2 = lax.top_k(s2, PEER_TOPK)
    cand = (t1[..., :, None] + t2[..., None, :]).reshape(n, PEER_HEADS, PEER_TOPK * PEER_TOPK)
    cidx = (i1[..., :, None] * N_KEYS + i2[..., None, :]).reshape(n, PEER_HEADS, PEER_TOPK * PEER_TOPK)
    top, sel = lax.top_k(cand, PEER_TOPK)
    idx = jnp.take_along_axis(cidx, sel, axis=-1)
    g = jax.nn.softmax(top, axis=-1)
    act = jax.nn.gelu(jnp.einsum('nd,nhkd->nhk', xb, u[idx]), approximate=False)
    return jnp.einsum('nhk,nhkd->nd', (g * act).astype(xb.dtype), v[idx])


def peer_ffn(x, wq, q_g, subkeys, u, v):
    B, T, D = x.shape
    n = B * T
    pad = (-n) % PEER_BLOCK
    xp = jnp.pad(x.reshape(n, D), ((0, pad), (0, 0)))
    out = lax.map(lambda xb: peer_block(xb, wq, q_g, subkeys, u, v), xp.reshape(-1, PEER_BLOCK, D))
    return out.reshape(-1, D)[:n].reshape(B, T, D)


def setup_inputs(seed: int = 0) -> dict:
    key = jax.random.key(seed)
    ks = jax.random.split(key, 32)
    f32 = jnp.float32
    n_pages = PAST_LEN // PAGE_SIZE
    n_used = DEC_BATCH * n_pages
    n_phys = n_used + max(1, n_used // 4)

    def nrm(k, shape, scale=1.0):
        return jax.random.normal(k, shape, f32) * scale

    page_table = jax.random.permutation(ks[7], n_phys)[:n_used].reshape(DEC_BATCH, n_pages).astype(jnp.int32)
    return {
        'x_prompt': nrm(ks[0], (BATCH, SEQ, D_MODEL)),
        'x_sample': nrm(ks[1], (DEC_BATCH, DEC_SEQ, D_MODEL)),
        'cache_diff_k': nrm(ks[2], (DEPTH, n_phys, PAGE_SIZE, H_A, 2 * DH_A)),
        'cache_diff_v': nrm(ks[3], (DEPTH, n_phys, PAGE_SIZE, H_A, DV_A)),
        'cache_fox_k': nrm(ks[4], (DEPTH, n_phys, PAGE_SIZE, H_B, DH_B)),
        'cache_fox_v': nrm(ks[5], (DEPTH, n_phys, PAGE_SIZE, H_B, DH_B)),
        'cache_fox_logf': jax.nn.log_sigmoid(2.0 + nrm(ks[6], (DEPTH, n_phys, PAGE_SIZE, H_B), 0.5)),
        'page_table': page_table,
        'w_in': nrm(ks[8], (DEPTH, D_MODEL, IN_COLS), D_MODEL ** -0.5),
        'b_forget': 2.0 + nrm(ks[9], (DEPTH, H_B), 0.5),
        'b_gate': nrm(ks[10], (DEPTH, 2 * D_MODEL), 0.02),
        'diff_lambda': nrm(ks[11], (DEPTH, 4, DH_A), 0.1),
        'diff_subln_g': 1.0 + nrm(ks[12], (DEPTH, DV_A), 0.02),
        'w_branch_a': nrm(ks[13], (DEPTH, W_A, D_MODEL), DEEPNORM_BETA * W_A ** -0.5),
        'w_branch_b': nrm(ks[14], (DEPTH, W_B, D_MODEL), DEEPNORM_BETA * W_B ** -0.5),
        'w_out': nrm(ks[15], (DEPTH, D_MODEL, D_MODEL), DEEPNORM_BETA * D_MODEL ** -0.5),
        'ln1_g': 1.0 + nrm(ks[16], (DEPTH, D_MODEL), 0.02),
        'ln1_b': nrm(ks[17], (DEPTH, D_MODEL), 0.02),
        'ln2_g': 1.0 + nrm(ks[18], (DEPTH, D_MODEL), 0.02),
        'ln2_b': nrm(ks[19], (DEPTH, D_MODEL), 0.02),
        'peer_wq': nrm(ks[20], (DEPTH, D_MODEL, PEER_HEADS * D_KEY), D_MODEL ** -0.5),
        'peer_q_g': 1.0 + nrm(ks[21], (DEPTH, D_KEY), 0.02),
        'peer_subkeys': nrm(ks[22], (DEPTH, PEER_HEADS, 2, N_KEYS, D_KEY // 2), (D_KEY // 2) ** -0.5),
        'peer_u': nrm(ks[23], (DEPTH, N_EXPERTS, D_MODEL), D_MODEL ** -0.5),
        'peer_v': nrm(ks[24], (DEPTH, N_EXPERTS, D_MODEL), DEEPNORM_BETA * PEER_HEADS ** -0.5),
    }


def reference(x_prompt, x_sample, cache_diff_k, cache_diff_v, cache_fox_k, cache_fox_v, cache_fox_logf,
              page_table, w_in, b_forget, b_gate, diff_lambda, diff_subln_g, w_branch_a, w_branch_b,
              w_out, ln1_g, ln1_b, ln2_g, ln2_b, peer_wq, peer_q_g, peer_subkeys, peer_u, peer_v):
    pos_p = jnp.arange(x_prompt.shape[1], dtype=jnp.int32)
    pos_s = PAST_LEN + jnp.arange(x_sample.shape[1], dtype=jnp.int32)

    def gather(cache_l):
        g = cache_l[page_table]
        return g.reshape((g.shape[0], g.shape[1] * g.shape[2]) + g.shape[3:])

    xp, xs = x_prompt, x_sample
    rows_p = [[], [], [], [], []]
    rows_s = [[], [], [], [], []]
    for l in range(DEPTH):
        lw = (w_in[l], b_forget[l], b_gate[l], diff_lambda[l], diff_subln_g[l],
              w_branch_a[l], w_branch_b[l], w_out[l])
        yp, new_p = token_mixer(xp, pos_p, l, *lw, None)
        past = (gather(cache_diff_k[l]), gather(cache_diff_v[l]), gather(cache_fox_k[l]),
                gather(cache_fox_v[l]), gather(cache_fox_logf[l]))
        ys, new_s = token_mixer(xs, pos_s, l, *lw, past)
        xp = layer_norm(DEEPNORM_ALPHA * xp + yp, ln1_g[l], ln1_b[l])
        xs = layer_norm(DEEPNORM_ALPHA * xs + ys, ln1_g[l], ln1_b[l])
        pw = (peer_wq[l], peer_q_g[l], peer_subkeys[l], peer_u[l], peer_v[l])
        xp = layer_norm(DEEPNORM_ALPHA * xp + peer_ffn(xp, *pw), ln2_g[l], ln2_b[l])
        xs = layer_norm(DEEPNORM_ALPHA * xs + peer_ffn(xs, *pw), ln2_g[l], ln2_b[l])
        for i in range(5):
            rows_p[i].append(new_p[i])
            rows_s[i].append(new_s[i])
    nk_p, nv_p, nfk_p, nfv_p, nfl_p = [jnp.stack(r, axis=0) for r in rows_p]
    nk_s, nv_s, nfk_s, nfv_s, nfl_s = [jnp.stack(r, axis=0) for r in rows_s]
    return (xp, xs, nk_p, nv_p, nfk_p, nfv_p, nfl_p, nk_s, nv_s, nfk_s, nfv_s, nfl_s)
```

```python
import functools
import math

import numpy as np
import jax
import jax.numpy as jnp
from jax import lax
from jax.experimental import pallas as pl
from jax.experimental.pallas import tpu as pltpu

F32 = jnp.float32
BF16 = jnp.bfloat16

H_A = 4
DH_A = 64
DV_A = 128
H_B = 8
DH_B = 64
ROPE_DIM = 16
ROPE_THETA = 500000.0
PEER_HEADS = 8
N_KEYS = 128
D_KEY = 256
PEER_TOPK = 16
LN_EPS = 1e-5
NEG_INF = -1e30
ATTN_SCALE = 0.125
LANES = 128
SUBLANES = 8
VMEM_LIMIT_BYTES = 56 * 1024 * 1024
N_CAND_ROWS = 152
N_EXTRACT = PEER_TOPK + 1
T_ROWS = 24
NEW_PAD = 16


def _cparams(sem):
    return pltpu.CompilerParams(dimension_semantics=sem, vmem_limit_bytes=VMEM_LIMIT_BYTES)


def _log_sigmoid(z):
    return jnp.minimum(z, 0.0) - jnp.log1p(jnp.exp(-jnp.abs(z)))


def _layer_norm(z, g, b):
    mu = jnp.mean(z, axis=-1, keepdims=True)
    zc = z - mu
    var = jnp.mean(zc * zc, axis=-1, keepdims=True)
    return zc * lax.rsqrt(var + LN_EPS) * g + b


def _dot(a, b):
    return jnp.dot(a, b, preferred_element_type=F32)


def _dot_nt(a, b):
    return lax.dot_general(a, b, (((1,), (1,)), ((), ())), preferred_element_type=F32)


def _inproj_body(x_ref, wqk_ref, wr_ref, wg_ref, wfl_ref, wflT_ref, bf_ref, bfT_ref, bg_ref,
                 rc_ref, rs1_ref, rs2_ref,
                 dq_ref, dk_ref, dkb_ref, dv_ref, dvb_ref, fq_ref, fk_ref, fkb_ref, fv_ref, fvb_ref,
                 lf_ref, lfT_ref, gate_ref):
    xb = x_ref[...].astype(BF16)
    qk = _dot(xb, wqk_ref[...])
    rc = rc_ref[...]
    rs1 = rs1_ref[...]
    rs2 = rs2_ref[...]
    for blk in range(8):
        d = qk[:, blk * LANES:(blk + 1) * LANES]
        r = d * rc + pltpu.roll(d, LANES - 8, 1) * rs1 + pltpu.roll(d, 8, 1) * rs2
        if blk < 4:
            dq_ref[:, blk * LANES:(blk + 1) * LANES] = (r * ATTN_SCALE).astype(BF16)
        else:
            c = (blk - 4) * LANES
            dk_ref[:, c:c + LANES] = r
            dkb_ref[:, c:c + LANES] = r.astype(BF16)
    rest = _dot(xb, wr_ref[...])
    dv = rest[:, 0:512]
    dv_ref[...] = dv
    dvb_ref[...] = dv.astype(BF16)
    fq_ref[...] = (rest[:, 512:1024] * ATTN_SCALE).astype(BF16)
    fk = rest[:, 1024:1536]
    fk_ref[...] = fk
    fkb_ref[...] = fk.astype(BF16)
    fv = rest[:, 1536:2048]
    fv_ref[...] = fv
    fvb_ref[...] = fv.astype(BF16)
    gl = _dot(xb, wg_ref[...]) + bg_ref[...]
    gate_ref[...] = jax.nn.sigmoid(gl).astype(BF16)
    fl = _dot(xb, wfl_ref[...])[:, 0:H_B] + bf_ref[...]
    lf_ref[...] = _log_sigmoid(fl)
    flT = _dot_nt(wflT_ref[...], xb)[0:H_B, :] + bfT_ref[...]
    lfT_ref[...] = _log_sigmoid(flT)


def _inproj(x, w, rope_tabs, tm):
    n = x.shape[0]
    rc, rs1, rs2 = rope_tabs
    nrep = rc.shape[0] // tm
    grid = (n // tm,)
    tok = lambda cols: pl.BlockSpec((tm, cols), lambda i: (i, 0))
    full = lambda a: pl.BlockSpec(a.shape, lambda i: (0,) * a.ndim)
    tab = pl.BlockSpec((tm, LANES), lambda i: (i % nrep, 0))
    out_shape = (
        jax.ShapeDtypeStruct((n, 512), BF16),
        jax.ShapeDtypeStruct((n, 512), F32),
        jax.ShapeDtypeStruct((n, 512), BF16),
        jax.ShapeDtypeStruct((n, 512), F32),
        jax.ShapeDtypeStruct((n, 512), BF16),
        jax.ShapeDtypeStruct((n, 512), BF16),
        jax.ShapeDtypeStruct((n, 512), F32),
        jax.ShapeDtypeStruct((n, 512), BF16),
        jax.ShapeDtypeStruct((n, 512), F32),
        jax.ShapeDtypeStruct((n, 512), BF16),
        jax.ShapeDtypeStruct((n, H_B), F32),
        jax.ShapeDtypeStruct((H_B, n), F32),
        jax.ShapeDtypeStruct((n, 2048), BF16),
    )
    out_specs = (tok(512),) * 10 + (tok(H_B), pl.BlockSpec((H_B, tm), lambda i: (0, i)), tok(2048))
    ins = (x, w['wqk'], w['wrest'], w['wg'], w['wfl'], w['wflT'], w['bf'], w['bfT'], w['bg'])
    in_specs = [tok(1024)] + [full(a) for a in ins[1:]] + [tab, tab, tab]
    return pl.pallas_call(
        _inproj_body, grid=grid, in_specs=in_specs, out_specs=out_specs, out_shape=out_shape,
        compiler_params=_cparams(("parallel",)), name="inproj",
    )(*ins, rc, rs1, rs2)


def _split3(a):
    hi = a.astype(BF16)
    r1 = a - hi.astype(F32)
    mid = r1.astype(BF16)
    lo = (r1 - mid.astype(F32)).astype(BF16)
    return hi, mid, lo


def _cumsum_body(lf_ref, lfT_ref, cum_ref, cumT_ref, *, t, cb):
    rows = lax.broadcasted_iota(jnp.int32, (cb, cb), 0)
    cols = lax.broadcasted_iota(jnp.int32, (cb, cb), 1)
    tri_l = jnp.where(rows >= cols, 1.0, 0.0).astype(BF16)
    tri_u = jnp.where(rows <= cols, 1.0, 0.0).astype(BF16)
    carry = jnp.zeros((1, H_B), F32)
    carry_t = jnp.zeros((H_B, 1), F32)
    for blk in range(t // cb):
        lb = lf_ref[blk * cb:(blk + 1) * cb, :]
        hi, mid, lo = _split3(lb)
        loc = _dot(tri_l, hi) + _dot(tri_l, mid) + _dot(tri_l, lo)
        cum_ref[blk * cb:(blk + 1) * cb, :] = loc + carry
        carry = carry + loc[cb - 1:cb, :]
        lbt = lfT_ref[:, blk * cb:(blk + 1) * cb]
        hi, mid, lo = _split3(lbt)
        loct = _dot(hi, tri_u) + _dot(mid, tri_u) + _dot(lo, tri_u)
        cumT_ref[:, blk * cb:(blk + 1) * cb] = loct + carry_t
        carry_t = carry_t + loct[:, cb - 1:cb]


def _cumsum(lf, lfT, b, t):
    cb = min(256, t)
    return pl.pallas_call(
        functools.partial(_cumsum_body, t=t, cb=cb), grid=(b,),
        in_specs=[pl.BlockSpec((t, H_B), lambda i: (i, 0)), pl.BlockSpec((H_B, t), lambda i: (0, i))],
        out_specs=(pl.BlockSpec((t, H_B), lambda i: (i, 0)), pl.BlockSpec((H_B, t), lambda i: (0, i))),
        out_shape=(jax.ShapeDtypeStruct(lf.shape, F32), jax.ShapeDtypeStruct(lfT.shape, F32)),
        compiler_params=_cparams(("parallel",)), name="logf_cumsum",
    )(lf, lfT)


def _diff_lambda(lp, lam_init):
    a = jnp.sum(lp[0:1, :] * lp[1:2, :], axis=1, keepdims=True)
    b = jnp.sum(lp[2:3, :] * lp[3:4, :], axis=1, keepdims=True)
    return jnp.exp(a) - jnp.exp(b) + lam_init


def _subln(a, g, lam_init):
    ms = jnp.mean(a * a, axis=-1, keepdims=True)
    return a * lax.rsqrt(ms + LN_EPS) * g * (1.0 - lam_init)


def _online_update(s, v, m, l, acc):
    m_new = jnp.maximum(m, jnp.max(s, axis=1, keepdims=True))
    alpha = jnp.exp(m - m_new)
    p = jnp.exp(s - m_new)
    l_new = alpha * l + jnp.sum(p, axis=1, keepdims=True)
    acc_new = alpha * acc + _dot(p.astype(BF16), v)
    return m_new, l_new, acc_new


def _attn_prompt_body(*refs, mode, tq, tk, lam_init):
    if mode == 'diff':
        q_ref, k_ref, v_ref, lam_ref, g_ref, o_ref = refs
    else:
        q_ref, k_ref, v_ref, cq_ref, ckT_ref, o_ref = refs
    qi = pl.program_id(2)
    q = q_ref[...]
    lane = lax.broadcasted_iota(jnp.int32, (tq, LANES), 1)
    zero = jnp.zeros_like(q)
    q2 = jnp.concatenate([jnp.where(lane < 64, q, zero), jnp.where(lane >= 64, q, zero)], axis=0)
    qpos = qi * tq + lax.broadcasted_iota(jnp.int32, (tq, tk), 0)
    qpos2 = jnp.concatenate([qpos, qpos], axis=0)
    kiota = lax.broadcasted_iota(jnp.int32, (2 * tq, tk), 1)
    if mode == 'fox':
        cq = cq_ref[0, 0]
        cq2 = jnp.concatenate([cq[:, 0:1], cq[:, 1:2]], axis=0)

    def step(j, carry):
        m, l, acc = carry
        k = k_ref[0, pl.ds(j * tk, tk), :]
        v = v_ref[0, pl.ds(j * tk, tk), :]
        s = _dot_nt(q2, k)
        if mode == 'fox':
            ck = ckT_ref[0, 0, :, pl.ds(j * tk, tk)]
            ck2 = jnp.concatenate([jnp.broadcast_to(ck[0:1, :], (tq, tk)),
                                   jnp.broadcast_to(ck[1:2, :], (tq, tk))], axis=0)
            s = s + (cq2 - ck2)
        s = jnp.where(j * tk + kiota <= qpos2, s, NEG_INF)
        return _online_update(s, v, m, l, acc)

    m0 = jnp.full((2 * tq, 1), -jnp.inf, F32)
    l0 = jnp.zeros((2 * tq, 1), F32)
    acc0 = jnp.zeros((2 * tq, LANES), F32)
    nkv = (qi * tq + tq + tk - 1) // tk
    m, l, acc = lax.fori_loop(0, nkv, step, (m0, l0, acc0))
    o = acc / l
    if mode == 'diff':
        lam = _diff_lambda(lam_ref[...], lam_init)
        a = o[0:tq] - lam * o[tq:2 * tq]
        o_ref[...] = _subln(a, g_ref[...], lam_init).astype(BF16)
    else:
        o_ref[...] = jnp.where(lane < 64, o[0:tq], o[tq:2 * tq]).astype(BF16)


def _attn_prompt(mode, q, k, v, extra, b, t, lam_init, tq, tk):
    n = b * t
    nq = t // tq
    k3 = k.reshape(b, t, 512)
    v3 = v.reshape(b, t, 512)
    qspec = pl.BlockSpec((tq, LANES), lambda bi, h, qi: (bi * nq + qi, h))
    kspec = pl.BlockSpec((1, t, LANES), lambda bi, h, qi: (bi, 0, h))
    if mode == 'diff':
        lam_p, g = extra
        ex_specs = [pl.BlockSpec(lam_p.shape, lambda bi, h, qi: (0, 0)),
                    pl.BlockSpec(g.shape, lambda bi, h, qi: (0, 0))]
        ex = (lam_p, g)
    else:
        cq, ckT = extra
        ex_specs = [pl.BlockSpec((1, 1, tq, 2), lambda bi, h, qi: (bi, h, qi, 0)),
                    pl.BlockSpec((1, 1, 2, t), lambda bi, h, qi: (bi, h, 0, 0))]
        ex = (cq, ckT)
    return pl.pallas_call(
        functools.partial(_attn_prompt_body, mode=mode, tq=tq, tk=tk, lam_init=lam_init),
        grid=(b, 4, nq), in_specs=[qspec, kspec, kspec] + ex_specs, out_specs=qspec,
        out_shape=jax.ShapeDtypeStruct((n, 512), BF16),
        compiler_params=_cparams(("parallel", "parallel", "arbitrary")), name="attn_prompt_" + mode,
    )(q, k3, v3, *ex)


def _decode_q_rows(q4, mode):
    rows = lax.broadcasted_iota(jnp.int32, (32, 512), 0)
    cols = lax.broadcasted_iota(jnp.int32, (32, 512), 1)
    if mode == 'diff':
        t_of = (rows // 4) % 4
        blk = 2 * (rows % 4) + rows // 16
    else:
        t_of = rows // 8
        blk = rows % 8
    qf = q4.astype(F32)
    rep = jnp.zeros((32, 512), F32)
    for t in range(4):
        rep = jnp.where(t_of == t, jnp.broadcast_to(qf[t:t + 1, :], (32, 512)), rep)
    return jnp.where(cols // 64 == blk, rep, 0.0).astype(BF16), t_of[:, 0:1]


def _attn_decode_body(*refs, mode, npg, ngroups, lam_init):
    pt_ref = refs[0]
    del pt_ref
    refs = refs[1:]
    q_ref, kn_ref, vn_ref = refs[0:3]
    if mode == 'diff':
        lam_ref, g_ref = refs[3:5]
        pos = 5
    else:
        lnT_ref = refs[3]
        pos = 4
        suf_refs = refs[pos:pos + npg]
        tot_refs = refs[pos + npg:pos + 2 * npg]
        pos += 2 * npg
    k_refs = refs[pos:pos + npg]
    v_refs = refs[pos + npg:pos + 2 * npg]
    pos += 2 * npg
    o_ref = refs[pos]
    qbd_ref, m_ref, l_ref, acc_ref = refs[pos + 1:pos + 5]
    if mode == 'fox':
        carry_ref, cn_ref = refs[pos + 5:pos + 7]
    jj = pl.program_id(1)

    @pl.when(jj == 0)
    def _init():
        qbd, _ = _decode_q_rows(q_ref[0], mode)
        qbd_ref[...] = qbd
        m_ref[...] = jnp.full((32, 1), -jnp.inf, F32)
        l_ref[...] = jnp.zeros((32, 1), F32)
        acc_ref[...] = jnp.zeros((32, 512), F32)
        if mode == 'fox':
            carry_ref[...] = jnp.zeros((SUBLANES, LANES), F32)
            lnT = lnT_ref[0]
            lane = lax.broadcasted_iota(jnp.int32, (SUBLANES, LANES), 1)
            cn = jnp.zeros((SUBLANES, LANES), F32)
            for t in range(4):
                cn = cn + jnp.where(lane >= t, jnp.broadcast_to(lnT[:, t:t + 1], (SUBLANES, LANES)), 0.0)
            cn_ref[...] = cn

    qbd = qbd_ref[...]
    kcat = jnp.concatenate([r[0, 0].astype(BF16) for r in k_refs], axis=0)
    vcat = jnp.concatenate([r[0, 0].astype(BF16) for r in v_refs], axis=0)
    s = _dot_nt(qbd, kcat)
    if mode == 'fox':
        carry = carry_ref[...]
        bias = [None] * npg
        for p in reversed(range(npg)):
            bias[p] = suf_refs[p][0, 0] + carry
            carry = carry + tot_refs[p][0, 0]
        carry_ref[...] = carry
        b8 = jnp.concatenate(bias, axis=1)
        cn = cn_ref[...]
        cn_col = jnp.concatenate([jnp.broadcast_to(cn[:, t:t + 1], (SUBLANES, 1)) for t in range(4)], axis=0)
        s = s + (jnp.concatenate([b8, b8, b8, b8], axis=0) + cn_col)
    m, l, acc = _online_update(s, vcat, m_ref[...], l_ref[...], acc_ref[...])
    m_ref[...] = m
    l_ref[...] = l
    acc_ref[...] = acc

    @pl.when(jj == ngroups - 1)
    def _finish():
        rows = lax.broadcasted_iota(jnp.int32, (32, NEW_PAD), 0)
        tcol = lax.broadcasted_iota(jnp.int32, (32, NEW_PAD), 1)
        t_of = (rows // 4) % 4 if mode == 'diff' else rows // 8
        sn = _dot_nt(qbd_ref[...], kn_ref[0])
        if mode == 'fox':
            cn = cn_ref[...]
            cn_col = jnp.concatenate([jnp.broadcast_to(cn[:, t:t + 1], (SUBLANES, 1)) for t in range(4)], axis=0)
            cn8 = cn[:, 0:NEW_PAD]
            sn = sn + (cn_col - jnp.concatenate([cn8, cn8, cn8, cn8], axis=0))
        sn = jnp.where(tcol <= t_of, sn, NEG_INF)
        m2, l2, acc2 = _online_update(sn, vn_ref[0], m_ref[...], l_ref[...], acc_ref[...])
        o = acc2 / l2
        r512 = lax.broadcasted_iota(jnp.int32, (32, 512), 0)
        c512 = lax.broadcasted_iota(jnp.int32, (32, 512), 1)
        if mode == 'diff':
            lam = _diff_lambda(lam_ref[...], lam_init)
            own = jnp.where(c512 // 128 == r512 % 4, o, 0.0)
            outs = []
            for t in range(4):
                o1 = jnp.sum(own[t * 4:t * 4 + 4], axis=0, keepdims=True)
                o2 = jnp.sum(own[16 + t * 4:16 + t * 4 + 4], axis=0, keepdims=True)
                outs.append(o1 - lam * o2)
            g = g_ref[...]
            for t in range(4):
                for h in range(H_A):
                    a = outs[t][:, h * 128:(h + 1) * 128]
                    o_ref[0, t:t + 1, h * 128:(h + 1) * 128] = _subln(a, g, lam_init).astype(BF16)
        else:
            own = jnp.where(c512 // 64 == r512 % 8, o, 0.0)
            for t in range(4):
                o_ref[0, t:t + 1, :] = jnp.sum(own[t * 8:t * 8 + 8], axis=0, keepdims=True).astype(BF16)


def _attn_decode(mode, layer, page_table, q, kn, vn, extra, cache_k, cache_v, lam_init, npg):
    nb, npages = page_table.shape
    ngroups = npages // npg
    rev = mode == 'fox'

    def page_idx(j, p):
        return ((ngroups - 1 - j) if rev else j) * npg + p

    def cache_spec(p):
        return pl.BlockSpec((1, 1, 128, 512), lambda b, j, pt: (layer, pt[b, page_idx(j, p)], 0, 0))

    small = lambda shape: pl.BlockSpec(shape, lambda b, j, pt: (b,) + (0,) * (len(shape) - 1))
    const = lambda a: pl.BlockSpec(a.shape, lambda b, j, pt: (0,) * a.ndim)
    ins = [q, kn, vn]
    in_specs = [small((1, NEW_PAD, 512))] * 3
    scratch = [pltpu.VMEM((32, 512), BF16), pltpu.VMEM((32, 1), F32), pltpu.VMEM((32, 1), F32),
               pltpu.VMEM((32, 512), F32)]
    if mode == 'diff':
        lam_p, g = extra
        ins += [lam_p, g]
        in_specs += [const(lam_p), const(g)]
    else:
        lnT, suf, tot = extra
        ins += [lnT] + [suf] * npg + [tot] * npg
        vspec = lambda p: pl.BlockSpec((1, 1, SUBLANES, LANES), lambda b, j, pt: (layer, pt[b, page_idx(j, p)], 0, 0))
        in_specs += [small((1, 8, 4))] + [vspec(p) for p in range(npg)] * 2
        scratch += [pltpu.VMEM((SUBLANES, LANES), F32), pltpu.VMEM((SUBLANES, LANES), F32)]
    ins += [cache_k] * npg + [cache_v] * npg
    in_specs += [cache_spec(p) for p in range(npg)] * 2
    grid_spec = pltpu.PrefetchScalarGridSpec(
        num_scalar_prefetch=1, grid=(nb, ngroups), in_specs=in_specs,
        out_specs=pl.BlockSpec((1, 4, 512), lambda b, j, pt: (b, 0, 0)), scratch_shapes=scratch)
    return pl.pallas_call(
        functools.partial(_attn_decode_body, mode=mode, npg=npg, ngroups=ngroups, lam_init=lam_init),
        grid_spec=grid_spec, out_shape=jax.ShapeDtypeStruct((nb, 4, 512), BF16),
        compiler_params=_cparams(("parallel", "arbitrary")), name="attn_decode_" + mode,
    )(page_table, *ins)


def _page_scan_body(x_ref, suf_ref, tot_ref):
    rows = lax.broadcasted_iota(jnp.int32, (LANES, LANES), 0)
    cols = lax.broadcasted_iota(jnp.int32, (LANES, LANES), 1)
    upper = jnp.where(rows > cols, 1.0, 0.0).astype(BF16)
    ones = jnp.ones((LANES, LANES), BF16)
    hi, mid, lo = _split3(x_ref[...])
    suf_ref[...] = _dot(hi, upper) + _dot(mid, upper) + _dot(lo, upper)
    tot_ref[...] = _dot(hi, ones) + _dot(mid, ones) + _dot(lo, ones)


def _page_scan(lfT_pages):
    rows = lfT_pages.shape[0]
    tr = 2048 if rows % 2048 == 0 else rows
    spec = pl.BlockSpec((tr, LANES), lambda i: (i, 0))
    return pl.pallas_call(
        _page_scan_body, grid=(rows // tr,), in_specs=[spec], out_specs=(spec, spec),
        out_shape=(jax.ShapeDtypeStruct(lfT_pages.shape, F32),) * 2,
        compiler_params=_cparams(("parallel",)), name="logf_page_scan",
    )(lfT_pages)


def _merge_body(x_ref, a_ref, f_ref, gate_ref, wba_ref, wbb_ref, wo_ref, g_ref, b_ref, o_ref, *, alpha):
    ba = _dot(a_ref[...], wba_ref[...])
    bb = _dot(f_ref[...], wbb_ref[...])
    gates = gate_ref[...].astype(F32)
    merged = gates[:, 0:1024] * ba + gates[:, 1024:2048] * bb
    y = _dot(merged.astype(BF16), wo_ref[...])
    o_ref[...] = _layer_norm(alpha * x_ref[...] + y, g_ref[...], b_ref[...])


def _merge(x, a, f, gates, w, alpha, tm):
    n = x.shape[0]
    tok = lambda cols: pl.BlockSpec((tm, cols), lambda i: (i, 0))
    full = lambda arr: pl.BlockSpec(arr.shape, lambda i: (0,) * arr.ndim)
    ws = (w['wba'], w['wbb'], w['wo'], w['ln1_g'], w['ln1_b'])
    return pl.pallas_call(
        functools.partial(_merge_body, alpha=alpha), grid=(n // tm,),
        in_specs=[tok(1024), tok(512), tok(512), tok(2048)] + [full(a_) for a_ in ws],
        out_specs=tok(1024), out_shape=jax.ShapeDtypeStruct((n, 1024), F32),
        compiler_params=_cparams(("parallel",)), name="merge",
    )(x, a, f, gates, *ws)


def _extract_maxima(work_ref, out_ref, side, n_rows):
    def it(a, carry):
        w = work_ref[0:n_rows, :]
        mx = jnp.max(w, axis=0, keepdims=True)
        out_ref[side, pl.ds(a, 1), :] = mx
        work_ref[0:n_rows, :] = jnp.where(w == mx, -jnp.inf, w)
        return carry
    lax.fori_loop(0, N_EXTRACT, it, 0)


def _peer_select_body(x_ref, wqT_ref, g_ref, sk_ref, th_ref, p_ref, s2_ref, q_ref,
                      qT_ref, t_ref, work_ref, *, tm):
    xb = x_ref[...].astype(BF16)
    qT_ref[...] = _dot_nt(wqT_ref[...], xb)
    t_ref[...] = jnp.full(t_ref.shape, -jnp.inf, F32)
    rows8 = lax.broadcasted_iota(jnp.int32, (SUBLANES, tm), 0)

    def head(h, carry):
        qh = qT_ref[pl.ds(pl.multiple_of(h * D_KEY, D_KEY), D_KEY), :]
        ms = jnp.mean(qh * qh, axis=0, keepdims=True)
        qn = (qh * lax.rsqrt(ms + LN_EPS) * g_ref[...]).astype(BF16)
        s1 = _dot(sk_ref[h, 0], qn[0:128])
        s2 = _dot(sk_ref[h, 1], qn[128:256])
        work_ref[0:N_KEYS, :] = s1
        _extract_maxima(work_ref, t_ref, 0, N_KEYS)
        work_ref[0:N_KEYS, :] = s2
        _extract_maxima(work_ref, t_ref, 1, N_KEYS)
        t2 = t_ref[1, 0:T_ROWS, :]
        off = 0
        for a in range(N_EXTRACT):
            nb = N_EXTRACT // (a + 1)
            ng = -(-nb // SUBLANES)
            c = t_ref[0, a:a + 1, :] + t2[0:ng * SUBLANES]
            if nb < SUBLANES:
                c = jnp.where(rows8 < nb, c, -jnp.inf)
            work_ref[off:off + ng * SUBLANES, :] = c
            t_ref[2, off:off + ng * SUBLANES, :] = c
            off += ng * SUBLANES
        _extract_maxima(work_ref, t_ref, 3, N_CAND_ROWS)
        c16 = t_ref[3, PEER_TOPK - 1:PEER_TOPK, :]
        c17 = t_ref[3, PEER_TOPK:PEER_TOPK + 1, :]
        tau = 0.5 * (c16 + c17)
        t1max = t_ref[0, 0:1, :]
        t2max = t_ref[1, 0:1, :]
        cand = t_ref[2, 0:N_CAND_ROWS, :]
        z = jnp.sum(jnp.where(cand > tau, jnp.exp(cand - (t1max + t2max)), 0.0), axis=0, keepdims=True)
        th_ref[h] = tau - s1
        p_ref[h] = jnp.exp(s1 - t1max) / z
        s2_ref[h] = s2
        q_ref[h] = jnp.exp(s2 - t2max)
        return carry

    lax.fori_loop(0, PEER_HEADS, head, 0)


def _peer_select(x, w, tm):
    n = x.shape[0]
    sel = pl.BlockSpec((PEER_HEADS, N_KEYS, tm), lambda i: (0, 0, i))
    full = lambda arr: pl.BlockSpec(arr.shape, lambda i: (0,) * arr.ndim)
    ws = (w['wqT'], w['qg'], w['sk'])
    t_rows = max(T_ROWS, N_CAND_ROWS)
    return pl.pallas_call(
        functools.partial(_peer_select_body, tm=tm), grid=(n // tm,),
        in_specs=[pl.BlockSpec((tm, 1024), lambda i: (i, 0))] + [full(a) for a in ws],
        out_specs=(sel,) * 4,
        out_shape=(jax.ShapeDtypeStruct((PEER_HEADS, N_KEYS, n), F32),) * 4,
        scratch_shapes=[pltpu.VMEM((PEER_HEADS * D_KEY, tm), F32),
                        pltpu.VMEM((4, t_rows, tm), F32),
                        pltpu.VMEM((N_CAND_ROWS, tm), F32)],
        compiler_params=_cparams(("parallel",)), name="peer_select",
    )(x, *ws)


def _peer_dense_body(x_ref, th_ref, p_ref, s2_ref, q_ref, u_ref, vT_ref, g_ref, b_ref, o_ref,
                     xT_ref, acc_ref, *, te, n_chunks, alpha):
    e = pl.program_id(1)

    @pl.when(e == 0)
    def _init():
        xT_ref[...] = x_ref[...].T.astype(BF16)
        acc_ref[...] = jnp.zeros(acc_ref.shape, F32)

    hT = _dot(u_ref[...], xT_ref[...])
    ys = []
    for ii in range(te // N_KEYS):
        row = e * (te // N_KEYS) + ii
        wgt = None
        for h in range(PEER_HEADS):
            th = th_ref[h, pl.ds(row, 1), :]
            pp = p_ref[h, pl.ds(row, 1), :]
            term = jnp.where(s2_ref[h] >= th, q_ref[h], 0.0) * pp
            wgt = term if wgt is None else wgt + term
        hh = hT[ii * N_KEYS:(ii + 1) * N_KEYS]
        act = 0.5 * hh * (1.0 + lax.erf(hh * math.sqrt(0.5)))
        ys.append((wgt * act).astype(BF16))
    yT = jnp.concatenate(ys, axis=0)
    acc_ref[...] += _dot(vT_ref[...], yT)

    @pl.when(e == n_chunks - 1)
    def _finish():
        z = alpha * x_ref[...] + acc_ref[...].T
        o_ref[...] = _layer_norm(z, g_ref[...], b_ref[...])


def _peer_dense(x, sel, w, alpha, tm, te):
    n = x.shape[0]
    n_exp = w['u'].shape[0]
    n_chunks = n_exp // te
    tok = pl.BlockSpec((tm, 1024), lambda i, e: (i, 0))
    selspec = pl.BlockSpec((PEER_HEADS, N_KEYS, tm), lambda i, e: (0, 0, i))
    full = lambda arr: pl.BlockSpec(arr.shape, lambda i, e: (0,) * arr.ndim)
    return pl.pallas_call(
        functools.partial(_peer_dense_body, te=te, n_chunks=n_chunks, alpha=alpha),
        grid=(n // tm, n_chunks),
        in_specs=[tok, selspec, selspec, selspec, selspec,
                  pl.BlockSpec((te, 1024), lambda i, e: (e, 0)),
                  pl.BlockSpec((1024, te), lambda i, e: (0, e)),
                  full(w['ln2_g']), full(w['ln2_b'])],
        out_specs=tok, out_shape=jax.ShapeDtypeStruct((n, 1024), F32),
        scratch_shapes=[pltpu.VMEM((1024, tm), BF16), pltpu.VMEM((1024, tm), F32)],
        compiler_params=_cparams(("parallel", "arbitrary")), name="peer_dense",
    )(x, *sel, w['u'], w['vT'], w['ln2_g'], w['ln2_b'])


def _rope_tables(pos):
    half = ROPE_DIM // 2
    inv_freq = ROPE_THETA ** (-jnp.arange(0, ROPE_DIM, 2, dtype=F32) / ROPE_DIM)
    ang = pos.astype(F32)[:, None] * inv_freq[None, :]
    cos = jnp.cos(ang)
    sin = jnp.sin(ang)
    n = pos.shape[0]
    one = jnp.ones((n, 64 - ROPE_DIM), F32)
    zero8 = jnp.zeros((n, half), F32)
    zero = jnp.zeros((n, 64 - ROPE_DIM), F32)
    rc = jnp.concatenate([cos, cos, one], axis=1)
    rs1 = jnp.concatenate([-sin, zero8, zero], axis=1)
    rs2 = jnp.concatenate([zero8, sin, zero], axis=1)
    return tuple(jnp.concatenate([a, a], axis=1) for a in (rc, rs1, rs2))


def _layer_weights(l, w_in, b_forget, b_gate, w_branch_a, w_branch_b, w_out, ln1_g, ln1_b, ln2_g, ln2_b,
                   peer_wq, peer_q_g, peer_subkeys, peer_u, peer_v):
    wi = w_in[l]
    wfl = wi[:, 3072:3080]
    return {
        'wqk': wi[:, 0:1024].astype(BF16),
        'wrest': wi[:, 1024:3072].astype(BF16),
        'wg': wi[:, 3080:5128].astype(BF16),
        'wfl': jnp.pad(wfl, ((0, 0), (0, LANES - H_B))).astype(BF16),
        'wflT': jnp.pad(wfl.T, ((0, 16 - H_B), (0, 0))).astype(BF16),
        'bf': b_forget[l].reshape(1, H_B),
        'bfT': b_forget[l].reshape(H_B, 1),
        'bg': b_gate[l].reshape(1, 2048),
        'wba': w_branch_a[l].astype(BF16),
        'wbb': w_branch_b[l].astype(BF16),
        'wo': w_out[l].astype(BF16),
        'ln1_g': ln1_g[l].reshape(1, 1024), 'ln1_b': ln1_b[l].reshape(1, 1024),
        'ln2_g': ln2_g[l].reshape(1, 1024), 'ln2_b': ln2_b[l].reshape(1, 1024),
        'wqT': peer_wq[l].T.astype(BF16),
        'qg': peer_q_g[l].reshape(D_KEY, 1),
        'sk': peer_subkeys[l].astype(BF16),
        'u': peer_u[l].astype(BF16),
        'vT': peer_v[l].T.astype(BF16),
    }


def kernel(x_prompt, x_sample, cache_diff_k, cache_diff_v, cache_fox_k, cache_fox_v, cache_fox_logf, page_table, w_in, b_forget, b_gate, diff_lambda, diff_subln_g, w_branch_a, w_branch_b, w_out, ln1_g, ln1_b, ln2_g, ln2_b, peer_wq, peer_q_g, peer_subkeys, peer_u, peer_v):
    depth = w_in.shape[0]
    b, t, d = x_prompt.shape
    nb, ts, _ = x_sample.shape
    past_len = page_table.shape[1] * cache_diff_k.shape[2]
    n_phys = cache_diff_k.shape[1]
    alpha = (2.0 * depth) ** 0.25
    assert ts == 4 and d == 1024

    xp = x_prompt.reshape(b * t, d)
    xs = x_sample.reshape(nb * ts, d)
    tabs_p = _rope_tables(jnp.arange(t, dtype=jnp.int32))
    tabs_s = _rope_tables(jnp.tile(past_len + jnp.arange(ts, dtype=jnp.int32), nb))

    ck = cache_diff_k.reshape(depth, n_phys, 128, 512)
    cv = cache_diff_v.reshape(depth, n_phys, 128, 512)
    fk_c = cache_fox_k.reshape(depth, n_phys, 128, 512)
    fv_c = cache_fox_v.reshape(depth, n_phys, 128, 512)
    lfT_pages = jnp.swapaxes(cache_fox_logf, 2, 3).reshape(depth * n_phys * H_B, 128)
    suf, tot = _page_scan(lfT_pages)
    suf = suf.reshape(depth, n_phys, H_B, 128)
    tot = tot.reshape(depth, n_phys, H_B, 128)

    tm_p = 256 if (b * t) % 256 == 0 else 128
    tq = min(256, t)
    rows_p = [[], [], [], [], []]
    rows_s = [[], [], [], [], []]
    pad4 = lambda a: jnp.pad(a.reshape(nb, ts, 512), ((0, 0), (0, NEW_PAD - ts), (0, 0)))
    for l in range(depth):
        w = _layer_weights(l, w_in, b_forget, b_gate, w_branch_a, w_branch_b, w_out, ln1_g, ln1_b,
                           ln2_g, ln2_b, peer_wq, peer_q_g, peer_subkeys, peer_u, peer_v)
        lam_init = 0.8 - 0.6 * math.exp(-0.3 * l)
        lam_p = diff_lambda[l]
        g_sub = diff_subln_g[l].reshape(1, DV_A)

        (dq, dk, dkb, dv, dvb, fq, fk, fkb, fv, fvb, lf, lfT, gates) = _inproj(xp, w, tabs_p, tm_p)
        cum, cumT = _cumsum(lf, lfT, b, t)
        cq = cum.reshape(b, t, 4, 2).transpose(0, 2, 1, 3)
        ckT = cumT.reshape(4, 2, b, t).transpose(2, 0, 1, 3)
        a_p = _attn_prompt('diff', dq, dkb, dvb, (lam_p, g_sub), b, t, lam_init, tq, tq)
        f_p = _attn_prompt('fox', fq, fkb, fvb, (cq, ckT), b, t, lam_init, tq, tq)
        x1 = _merge(xp, a_p, f_p, gates, w, alpha, 512 if (b * t) % 512 == 0 else 128)
        tm_e = 512 if (b * t) % 512 == 0 else 128
        sel = _peer_select(x1, w, tm_e)
        xp = _peer_dense(x1, sel, w, alpha, tm_e, 512)
        for i, r in enumerate((dk, dv, fk, fv, lf)):
            rows_p[i].append(r)

        (dq, dk, dkb, dv, dvb, fq, fk, fkb, fv, fvb, lf, lfT, gates) = _inproj(xs, w, tabs_s, nb * ts)
        a_s = _attn_decode('diff', l, page_table, pad4(dq), pad4(dkb), pad4(dvb), (lam_p, g_sub),
                           ck, cv, lam_init, 8)
        lnT = lfT.reshape(H_B, nb, ts).transpose(1, 0, 2)
        f_s = _attn_decode('fox', l, page_table, pad4(fq), pad4(fkb), pad4(fvb), (lnT, suf, tot),
                           fk_c, fv_c, lam_init, 8)
        x1 = _merge(xs, a_s.reshape(nb * ts, 512), f_s.reshape(nb * ts, 512), gates, w, alpha, nb * ts)
        sel = _peer_select(x1, w, nb * ts)
        xs = _peer_dense(x1, sel, w, alpha, nb * ts, 512)
        for i, r in enumerate((dk, dv, fk, fv, lf)):
            rows_s[i].append(r)

    def stack(rows, lead, tail):
        return jnp.stack(rows, axis=0).reshape((depth,) + lead + tail)

    lead_p, lead_s = (b, t), (nb, ts)
    tails = ((H_A, 2 * DH_A), (H_A, DV_A), (H_B, DH_B), (H_B, DH_B), (H_B,))
    outs_p = [stack(rows_p[i], lead_p, tails[i]) for i in range(5)]
    outs_s = [stack(rows_s[i], lead_s, tails[i]) for i in range(5)]
    return (xp.reshape(b, t, d), xs.reshape(nb, ts, d), *outs_p, *outs_s)
```

```python
import functools
import math

import jax
import jax.numpy as jnp
from jax import lax
from jax.experimental import pallas as pl
from jax.experimental.pallas import tpu as pltpu

F32 = jnp.float32
BF16 = jnp.bfloat16

H_A = 4
DH_A = 64
DV_A = 128
H_B = 8
DH_B = 64
ROPE_DIM = 16
ROPE_THETA = 500000.0
PEER_HEADS = 8
N_KEYS = 128
D_KEY = 256
PEER_TOPK = 16
LN_EPS = 1e-5
NEG_INF = -1e30
ATTN_SCALE = 0.125
LANES = 128
SUBLANES = 8
VMEM_LIMIT_BYTES = 56 * 1024 * 1024
N_CAND_ROWS = 152
N_EXTRACT = PEER_TOPK + 1
T_ROWS = 24
NEW_PAD = 16
DEC_ROWS = 16


def _cparams(sem):
    return pltpu.CompilerParams(dimension_semantics=sem, vmem_limit_bytes=VMEM_LIMIT_BYTES)


def _log_sigmoid(z):
    return jnp.minimum(z, 0.0) - jnp.log1p(jnp.exp(-jnp.abs(z)))


def _layer_norm(z, g, b):
    mu = jnp.mean(z, axis=-1, keepdims=True)
    zc = z - mu
    var = jnp.mean(zc * zc, axis=-1, keepdims=True)
    return zc * lax.rsqrt(var + LN_EPS) * g + b


def _dot(a, b):
    return jnp.dot(a, b, preferred_element_type=F32)


def _dot_nt(a, b):
    return lax.dot_general(a, b, (((1,), (1,)), ((), ())), preferred_element_type=F32)


def _inproj_body(x_ref, wqk_ref, wr_ref, wg_ref, wfl_ref, wflT_ref, bf_ref, bfT_ref, bg_ref,
                 rc_ref, rs1_ref, rs2_ref,
                 dq_ref, dk_ref, dkb_ref, dv_ref, dvb_ref, fq_ref, fk_ref, fkb_ref, fv_ref, fvb_ref,
                 lf_ref, lfT_ref, gate_ref):
    xb = x_ref[...].astype(BF16)
    qk = _dot(xb, wqk_ref[...])
    rc = rc_ref[...]
    rs1 = rs1_ref[...]
    rs2 = rs2_ref[...]
    for blk in range(8):
        d = qk[:, blk * LANES:(blk + 1) * LANES]
        r = d * rc + pltpu.roll(d, LANES - 8, 1) * rs1 + pltpu.roll(d, 8, 1) * rs2
        if blk < 4:
            dq_ref[:, blk * LANES:(blk + 1) * LANES] = (r * ATTN_SCALE).astype(BF16)
        else:
            c = (blk - 4) * LANES
            dk_ref[:, c:c + LANES] = r
            dkb_ref[:, c:c + LANES] = r.astype(BF16)
    rest = _dot(xb, wr_ref[...])
    dv = rest[:, 0:512]
    dv_ref[...] = dv
    dvb_ref[...] = dv.astype(BF16)
    fq_ref[...] = (rest[:, 512:1024] * ATTN_SCALE).astype(BF16)
    fk = rest[:, 1024:1536]
    fk_ref[...] = fk
    fkb_ref[...] = fk.astype(BF16)
    fv = rest[:, 1536:2048]
    fv_ref[...] = fv
    fvb_ref[...] = fv.astype(BF16)
    gl = _dot(xb, wg_ref[...]) + bg_ref[...]
    gate_ref[...] = jax.nn.sigmoid(gl).astype(BF16)
    fl = _dot(xb, wfl_ref[...])[:, 0:H_B] + bf_ref[...]
    lf_ref[...] = _log_sigmoid(fl)
    flT = _dot_nt(wflT_ref[...], xb)[0:H_B, :] + bfT_ref[...]
    lfT_ref[...] = _log_sigmoid(flT)


def _inproj(x, w, rope_tabs, tm):
    n = x.shape[0]
    rc, rs1, rs2 = rope_tabs
    nrep = rc.shape[0] // tm
    grid = (n // tm,)
    tok = lambda cols: pl.BlockSpec((tm, cols), lambda i: (i, 0))
    full = lambda a: pl.BlockSpec(a.shape, lambda i: (0,) * a.ndim)
    tab = pl.BlockSpec((tm, LANES), lambda i: (i % nrep, 0))
    out_shape = (
        jax.ShapeDtypeStruct((n, 512), BF16),
        jax.ShapeDtypeStruct((n, 512), F32),
        jax.ShapeDtypeStruct((n, 512), BF16),
        jax.ShapeDtypeStruct((n, 512), F32),
        jax.ShapeDtypeStruct((n, 512), BF16),
        jax.ShapeDtypeStruct((n, 512), BF16),
        jax.ShapeDtypeStruct((n, 512), F32),
        jax.ShapeDtypeStruct((n, 512), BF16),
        jax.ShapeDtypeStruct((n, 512), F32),
        jax.ShapeDtypeStruct((n, 512), BF16),
        jax.ShapeDtypeStruct((n, H_B), F32),
        jax.ShapeDtypeStruct((H_B, n), F32),
        jax.ShapeDtypeStruct((n, 2048), BF16),
    )
    out_specs = (tok(512),) * 10 + (tok(H_B), pl.BlockSpec((H_B, tm), lambda i: (0, i)), tok(2048))
    ins = (x, w['wqk'], w['wrest'], w['wg'], w['wfl'], w['wflT'], w['bf'], w['bfT'], w['bg'])
    in_specs = [tok(1024)] + [full(a) for a in ins[1:]] + [tab, tab, tab]
    return pl.pallas_call(
        _inproj_body, grid=grid, in_specs=in_specs, out_specs=out_specs, out_shape=out_shape,
        compiler_params=_cparams(("parallel",)), name="inproj",
    )(*ins, rc, rs1, rs2)


def _split3(a):
    hi = a.astype(BF16)
    r1 = a - hi.astype(F32)
    mid = r1.astype(BF16)
    lo = (r1 - mid.astype(F32)).astype(BF16)
    return hi, mid, lo


def _cumsum_body(lf_ref, lfT_ref, cum_ref, cumT_ref, *, t, cb):
    rows = lax.broadcasted_iota(jnp.int32, (cb, cb), 0)
    cols = lax.broadcasted_iota(jnp.int32, (cb, cb), 1)
    tri_l = jnp.where(rows >= cols, 1.0, 0.0).astype(BF16)
    tri_u = jnp.where(rows <= cols, 1.0, 0.0).astype(BF16)
    carry = jnp.zeros((1, H_B), F32)
    carry_t = jnp.zeros((H_B, 1), F32)
    for blk in range(t // cb):
        lb = lf_ref[blk * cb:(blk + 1) * cb, :]
        hi, mid, lo = _split3(lb)
        loc = _dot(tri_l, hi) + _dot(tri_l, mid) + _dot(tri_l, lo)
        cum_ref[blk * cb:(blk + 1) * cb, :] = loc + carry
        carry = carry + loc[cb - 1:cb, :]
        lbt = lfT_ref[:, blk * cb:(blk + 1) * cb]
        hi, mid, lo = _split3(lbt)
        loct = _dot(hi, tri_u) + _dot(mid, tri_u) + _dot(lo, tri_u)
        cumT_ref[:, blk * cb:(blk + 1) * cb] = loct + carry_t
        carry_t = carry_t + loct[:, cb - 1:cb]


def _cumsum(lf, lfT, b, t):
    cb = min(256, t)
    return pl.pallas_call(
        functools.partial(_cumsum_body, t=t, cb=cb), grid=(b,),
        in_specs=[pl.BlockSpec((t, H_B), lambda i: (i, 0)), pl.BlockSpec((H_B, t), lambda i: (0, i))],
        out_specs=(pl.BlockSpec((t, H_B), lambda i: (i, 0)), pl.BlockSpec((H_B, t), lambda i: (0, i))),
        out_shape=(jax.ShapeDtypeStruct(lf.shape, F32), jax.ShapeDtypeStruct(lfT.shape, F32)),
        compiler_params=_cparams(("parallel",)), name="logf_cumsum",
    )(lf, lfT)


def _diff_lambda(lp, lam_init):
    a = jnp.sum(lp[0:1, :] * lp[1:2, :], axis=1, keepdims=True)
    b = jnp.sum(lp[2:3, :] * lp[3:4, :], axis=1, keepdims=True)
    return jnp.exp(a) - jnp.exp(b) + lam_init


def _subln(a, g, lam_init):
    ms = jnp.mean(a * a, axis=-1, keepdims=True)
    return a * lax.rsqrt(ms + LN_EPS) * g * (1.0 - lam_init)


def _attn_prompt_body(*refs, mode, tq, lam_init):
    if mode == 'diff':
        q_ref, k_ref, v_ref, lam_ref, g_ref, o_ref, s_ref, mx_ref, ls_ref, acc_ref = refs
    else:
        q_ref, k_ref, v_ref, cq_ref, ckT_ref, o_ref, s_ref, mx_ref, ls_ref, acc_ref = refs
    tk = tq
    qi = pl.program_id(2)
    q = q_ref[...]
    lane = lax.broadcasted_iota(jnp.int32, (tq, LANES), 1)
    zero = jnp.zeros_like(q)
    q2 = jnp.concatenate([jnp.where(lane < 64, q, zero), jnp.where(lane >= 64, q, zero)], axis=0)
    if mode == 'fox':
        cq = cq_ref[0, 0]
        cq2 = jnp.concatenate([cq[:, 0:1], cq[:, 1:2]], axis=0)

    def scores(j):
        k = k_ref[0, pl.ds(pl.multiple_of(j * tk, tk), tk), :]
        s = _dot_nt(q2, k)
        if mode == 'fox':
            ck = ckT_ref[0, 0, :, pl.ds(pl.multiple_of(j * tk, tk), tk)]
            ck2 = jnp.concatenate([jnp.broadcast_to(ck[0:1, :], (tq, tk)),
                                   jnp.broadcast_to(ck[1:2, :], (tq, tk))], axis=0)
            s = s + (cq2 - ck2)
        return s

    def fold_max(s):
        m = mx_ref[...]
        for c0 in range(0, tk, LANES):
            m = jnp.maximum(m, s[:, c0:c0 + LANES])
        mx_ref[...] = m

    mx_ref[...] = jnp.full(mx_ref.shape, -jnp.inf, F32)

    def sweep1(j, carry):
        s = scores(j)
        s_ref[j] = s
        fold_max(s)
        return carry

    lax.fori_loop(0, qi, sweep1, 0)
    row = lax.broadcasted_iota(jnp.int32, (tq, tk), 0)
    col = lax.broadcasted_iota(jnp.int32, (tq, tk), 1)
    causal = jnp.concatenate([col <= row, col <= row], axis=0)
    s = jnp.where(causal, scores(qi), NEG_INF)
    s_ref[qi] = s
    fold_max(s)
    m = jnp.max(mx_ref[...], axis=1, keepdims=True)
    mx_ref[...] = jnp.broadcast_to(m, mx_ref.shape)
    ls_ref[...] = jnp.zeros(ls_ref.shape, F32)
    acc_ref[...] = jnp.zeros(acc_ref.shape, F32)

    def sweep2(j, carry):
        mb = mx_ref[...]
        p = jnp.exp(s_ref[j] - jnp.concatenate([mb] * (tk // LANES), axis=1))
        ls = ls_ref[...]
        for c0 in range(0, tk, LANES):
            ls = ls + p[:, c0:c0 + LANES]
        ls_ref[...] = ls
        v = v_ref[0, pl.ds(pl.multiple_of(j * tk, tk), tk), :]
        acc_ref[...] += _dot(p.astype(BF16), v)
        return carry

    lax.fori_loop(0, qi + 1, sweep2, 0)
    o = acc_ref[...] / jnp.sum(ls_ref[...], axis=1, keepdims=True)
    if mode == 'diff':
        lam = _diff_lambda(lam_ref[...], lam_init)
        a = o[0:tq] - lam * o[tq:2 * tq]
        o_ref[...] = _subln(a, g_ref[...], lam_init).astype(BF16)
    else:
        o_ref[...] = jnp.where(lane < 64, o[0:tq], o[tq:2 * tq]).astype(BF16)


def _attn_prompt(mode, q, k, v, extra, b, t, lam_init, tq):
    n = b * t
    nq = t // tq
    k3 = k.reshape(b, t, 512)
    v3 = v.reshape(b, t, 512)
    qspec = pl.BlockSpec((tq, LANES), lambda bi, h, qi: (bi * nq + qi, h))
    kspec = pl.BlockSpec((1, t, LANES), lambda bi, h, qi: (bi, 0, h))
    if mode == 'diff':
        lam_p, g = extra
        ex_specs = [pl.BlockSpec(lam_p.shape, lambda bi, h, qi: (0, 0)),
                    pl.BlockSpec(g.shape, lambda bi, h, qi: (0, 0))]
        ex = (lam_p, g)
    else:
        cq, ckT = extra
        ex_specs = [pl.BlockSpec((1, 1, tq, 2), lambda bi, h, qi: (bi, h, qi, 0)),
                    pl.BlockSpec((1, 1, 2, t), lambda bi, h, qi: (bi, h, 0, 0))]
        ex = (cq, ckT)
    scratch = [pltpu.VMEM((nq, 2 * tq, tq), F32), pltpu.VMEM((2 * tq, LANES), F32),
               pltpu.VMEM((2 * tq, LANES), F32), pltpu.VMEM((2 * tq, LANES), F32)]
    return pl.pallas_call(
        functools.partial(_attn_prompt_body, mode=mode, tq=tq, lam_init=lam_init),
        grid=(b, 4, nq), in_specs=[qspec, kspec, kspec] + ex_specs, out_specs=qspec,
        out_shape=jax.ShapeDtypeStruct((n, 512), BF16), scratch_shapes=scratch,
        compiler_params=_cparams(("parallel", "parallel", "arbitrary")), name="attn_prompt_" + mode,
    )(q, k3, v3, *ex)


def _attn_decode_body(*refs, mode, npg, ngroups, lam_init):
    refs = refs[1:]
    diff = mode == 'diff'
    nh = H_A if diff else H_B
    q_ref, kn_ref, vn_ref = refs[0:3]
    if diff:
        lam_ref, g_ref = refs[3:5]
        pos = 5
    else:
        lnat_ref, lrow_ref = refs[3:5]
        suf_refs = refs[5:5 + npg]
        tot_refs = refs[5 + npg:5 + 2 * npg]
        pos = 5 + 2 * npg
    k_refs = refs[pos:pos + npg]
    v_refs = refs[pos + npg:pos + 2 * npg]
    pos += 2 * npg
    o_ref, m_ref, l_ref, acc_ref = refs[pos:pos + 4]
    if not diff:
        carry_ref, cnc_ref, cnr_ref = refs[pos + 4:pos + 7]
    jj = pl.program_id(1)

    @pl.when(jj == 0)
    def _init():
        m_ref[...] = jnp.full(m_ref.shape, -jnp.inf, F32)
        l_ref[...] = jnp.zeros(l_ref.shape, F32)
        acc_ref[...] = jnp.zeros(acc_ref.shape, F32)
        if not diff:
            carry_ref[...] = jnp.zeros((SUBLANES, LANES), F32)
            ln = lnat_ref[0]
            rows = lax.broadcasted_iota(jnp.int32, (DEC_ROWS, LANES), 0)
            cn = jnp.zeros((DEC_ROWS, LANES), F32)
            for t in range(4):
                cn = cn + jnp.where(rows >= t, jnp.broadcast_to(ln[t:t + 1, :], (DEC_ROWS, LANES)), 0.0)
            for h in range(nh):
                cnc_ref[h] = jnp.broadcast_to(cn[:, h:h + 1], (DEC_ROWS, LANES))
            lr = lrow_ref[0]
            lanes = lax.broadcasted_iota(jnp.int32, (SUBLANES, LANES), 1)
            cr = jnp.zeros((SUBLANES, LANES), F32)
            for t in range(4):
                cr = cr + jnp.where(lanes >= t, jnp.broadcast_to(lr[:, t:t + 1], (SUBLANES, LANES)), 0.0)
            cnr_ref[...] = cr

    def update(s_heads, pv_fns):
        s = jnp.concatenate(s_heads, axis=0)
        m_old = m_ref[...]
        m_new = jnp.maximum(m_old, jnp.max(s, axis=1, keepdims=True))
        alpha = jnp.exp(m_old - m_new)
        p = jnp.exp(s - m_new)
        l_ref[...] = alpha * l_ref[...] + jnp.sum(p, axis=1, keepdims=True)
        pb = p.astype(BF16)
        pv = jnp.concatenate([fn(pb[h * DEC_ROWS:(h + 1) * DEC_ROWS]) for h, fn in enumerate(pv_fns)], axis=0)
        acc_ref[...] = alpha * acc_ref[...] + pv
        m_ref[...] = m_new

    if not diff:
        carry = carry_ref[...]
        bias = [None] * npg
        for p in reversed(range(npg)):
            bias[p] = suf_refs[p][0, 0] + carry
            carry = carry + tot_refs[p][0, 0]
        carry_ref[...] = carry

    s_heads, pv_fns = [], []
    for h in range(nh):
        if diff:
            kc = jnp.concatenate([r[0, 0, pl.ds(h, 128, stride=H_A), :].astype(BF16) for r in k_refs], axis=0)
            vc = jnp.concatenate([r[0, 0, pl.ds(h, 128, stride=H_A), :].astype(BF16) for r in v_refs], axis=0)
            s_heads.append(_dot_nt(q_ref[0, h], kc))
            pv_fns.append(lambda p, vc=vc: _dot(p, vc))
        else:
            kc = jnp.concatenate([r[0, 0, h].astype(BF16) for r in k_refs], axis=1)
            vc = jnp.concatenate([r[0, 0, h].astype(BF16) for r in v_refs], axis=1)
            bias_h = jnp.concatenate([b_[h:h + 1, :] for b_ in bias], axis=1)
            cncol = cnc_ref[h]
            s_heads.append(_dot(q_ref[0, h], kc) + (bias_h + jnp.concatenate([cncol] * npg, axis=1)))
            pv_fns.append(lambda p, vc=vc: _dot_nt(p, vc))
    update(s_heads, pv_fns)

    @pl.when(jj == ngroups - 1)
    def _finish():
        ncol = NEW_PAD if diff else LANES
        rows = lax.broadcasted_iota(jnp.int32, (DEC_ROWS, ncol), 0)
        tcol = lax.broadcasted_iota(jnp.int32, (DEC_ROWS, ncol), 1)
        t_of = jnp.minimum(rows % 4 if diff else rows, 3)
        visible = tcol <= t_of
        if diff:
            lam = _diff_lambda(lam_ref[...], lam_init)
            g = g_ref[...]
        sn_heads, pvn_fns = [], []
        for h in range(nh):
            vn = vn_ref[0, h]
            if diff:
                sn = _dot_nt(q_ref[0, h], kn_ref[0, h])
                pvn_fns.append(lambda p, vn=vn: _dot(p, vn))
            else:
                sn = _dot(q_ref[0, h], kn_ref[0, h])
                sn = sn + (cnc_ref[h] - cnr_ref[h:h + 1, :])
                pvn_fns.append(lambda p, vn=vn: _dot_nt(p, vn))
            sn_heads.append(jnp.where(visible, sn, NEG_INF))
        update(sn_heads, pvn_fns)
        o_all = acc_ref[...] / l_ref[...]
        for h in range(nh):
            o = o_all[h * DEC_ROWS:(h + 1) * DEC_ROWS]
            if diff:
                a = o[0:4] - lam * o[4:8]
                o_ref[0, :, h * DV_A:(h + 1) * DV_A] = _subln(a, g, lam_init)
            else:
                o_ref[0, :, h * DH_B:(h + 1) * DH_B] = o[0:4]


def _attn_decode(mode, layer, page_table, q, kn, vn, extra, cache_k, cache_v, lam_init, npg):
    nb, npages = page_table.shape
    ngroups = npages // npg
    diff = mode == 'diff'
    nh = H_A if diff else H_B
    dv = DV_A if diff else DH_B

    def page_idx(j, p):
        return (j if diff else (ngroups - 1 - j)) * npg + p

    def cache_spec(p):
        if diff:
            return pl.BlockSpec((1, 1, 4 * 128, 128), lambda b, j, pt: (layer, pt[b, page_idx(j, p)], 0, 0))
        return pl.BlockSpec((1, 1, H_B, DH_B, 128), lambda b, j, pt: (layer, pt[b, page_idx(j, p)], 0, 0, 0))

    small = lambda a: pl.BlockSpec((1,) + a.shape[1:], lambda b, j, pt: (b,) + (0,) * (a.ndim - 1))
    const = lambda a: pl.BlockSpec(a.shape, lambda b, j, pt: (0,) * a.ndim)
    ins = [q, kn, vn]
    in_specs = [small(q), small(kn), small(vn)]
    scratch = [pltpu.VMEM((nh * DEC_ROWS, 1), F32), pltpu.VMEM((nh * DEC_ROWS, 1), F32),
               pltpu.VMEM((nh * DEC_ROWS, dv), F32)]
    if diff:
        lam_p, g = extra
        ins += [lam_p, g]
        in_specs += [const(lam_p), const(g)]
    else:
        lnat, lrow, suf, tot = extra
        ins += [lnat, lrow] + [suf] * npg + [tot] * npg
        vspec = lambda p: pl.BlockSpec((1, 1, SUBLANES, LANES), lambda b, j, pt: (layer, pt[b, page_idx(j, p)], 0, 0))
        in_specs += [small(lnat), small(lrow)] + [vspec(p) for p in range(npg)] * 2
        scratch += [pltpu.VMEM((SUBLANES, LANES), F32), pltpu.VMEM((nh, DEC_ROWS, LANES), F32),
                    pltpu.VMEM((SUBLANES, LANES), F32)]
    ins += [cache_k] * npg + [cache_v] * npg
    in_specs += [cache_spec(p) for p in range(npg)] * 2
    grid_spec = pltpu.PrefetchScalarGridSpec(
        num_scalar_prefetch=1, grid=(nb, ngroups), in_specs=in_specs,
        out_specs=pl.BlockSpec((1, 4, 512), lambda b, j, pt: (b, 0, 0)), scratch_shapes=scratch)
    return pl.pallas_call(
        functools.partial(_attn_decode_body, mode=mode, npg=npg, ngroups=ngroups, lam_init=lam_init),
        grid_spec=grid_spec, out_shape=jax.ShapeDtypeStruct((nb, 4, 512), F32),
        compiler_params=_cparams(("parallel", "arbitrary")), name="attn_decode_" + mode,
    )(page_table, *ins)


def _decode_operands(mode, q, k, v, nb):
    if mode == 'diff':
        q4 = q.reshape(nb, 4, H_A, 128).transpose(0, 2, 1, 3)
        lane = jnp.arange(128) < DH_A
        qq = jnp.concatenate([jnp.where(lane, q4, 0), jnp.where(lane, 0, q4)], axis=2)
        qq = jnp.pad(qq, ((0, 0), (0, 0), (0, DEC_ROWS - 8), (0, 0)))
        rows = lambda a: jnp.pad(a.reshape(nb, 4, H_A, 128).transpose(0, 2, 1, 3),
                                 ((0, 0), (0, 0), (0, NEW_PAD - 4), (0, 0)))
        return qq, rows(k), rows(v)
    q4 = q.reshape(nb, 4, H_B, DH_B).transpose(0, 2, 1, 3)
    qq = jnp.pad(q4, ((0, 0), (0, 0), (0, DEC_ROWS - 4), (0, 0)))
    cols = lambda a: jnp.pad(a.reshape(nb, 4, H_B, DH_B).transpose(0, 2, 3, 1),
                             ((0, 0), (0, 0), (0, 0), (0, LANES - 4)))
    return qq, cols(k), cols(v)


def _page_scan_body(x_ref, suf_ref, tot_ref):
    rows = lax.broadcasted_iota(jnp.int32, (LANES, LANES), 0)
    cols = lax.broadcasted_iota(jnp.int32, (LANES, LANES), 1)
    upper = jnp.where(rows > cols, 1.0, 0.0).astype(BF16)
    ones = jnp.ones((LANES, LANES), BF16)
    hi, mid, lo = _split3(x_ref[...])
    suf_ref[...] = _dot(hi, upper) + _dot(mid, upper) + _dot(lo, upper)
    tot_ref[...] = _dot(hi, ones) + _dot(mid, ones) + _dot(lo, ones)


def _page_scan(lfT_pages):
    rows = lfT_pages.shape[0]
    tr = 2048 if rows % 2048 == 0 else rows
    spec = pl.BlockSpec((tr, LANES), lambda i: (i, 0))
    return pl.pallas_call(
        _page_scan_body, grid=(rows // tr,), in_specs=[spec], out_specs=(spec, spec),
        out_shape=(jax.ShapeDtypeStruct(lfT_pages.shape, F32),) * 2,
        compiler_params=_cparams(("parallel",)), name="logf_page_scan",
    )(lfT_pages)


def _merge_body(x_ref, a_ref, f_ref, gate_ref, wba_ref, wbb_ref, wo_ref, g_ref, b_ref, o_ref, *, alpha):
    ba = _dot(a_ref[...].astype(BF16), wba_ref[...])
    bb = _dot(f_ref[...].astype(BF16), wbb_ref[...])
    gates = gate_ref[...].astype(F32)
    merged = gates[:, 0:1024] * ba + gates[:, 1024:2048] * bb
    y = _dot(merged.astype(BF16), wo_ref[...])
    o_ref[...] = _layer_norm(alpha * x_ref[...] + y, g_ref[...], b_ref[...])


def _merge(x, a, f, gates, w, alpha, tm):
    n = x.shape[0]
    tok = lambda cols: pl.BlockSpec((tm, cols), lambda i: (i, 0))
    full = lambda arr: pl.BlockSpec(arr.shape, lambda i: (0,) * arr.ndim)
    ws = (w['wba'], w['wbb'], w['wo'], w['ln1_g'], w['ln1_b'])
    return pl.pallas_call(
        functools.partial(_merge_body, alpha=alpha), grid=(n // tm,),
        in_specs=[tok(1024), tok(512), tok(512), tok(2048)] + [full(a_) for a_ in ws],
        out_specs=tok(1024), out_shape=jax.ShapeDtypeStruct((n, 1024), F32),
        compiler_params=_cparams(("parallel",)), name="merge",
    )(x, a, f, gates, *ws)


def _extract_maxima(work_ref, out_ref, side, n_rows):
    def it(a, carry):
        w = work_ref[0:n_rows, :]
        mx = jnp.max(w, axis=0, keepdims=True)
        out_ref[side, pl.ds(a, 1), :] = mx
        work_ref[0:n_rows, :] = jnp.where(w == mx, -jnp.inf, w)
        return carry
    lax.fori_loop(0, N_EXTRACT, it, 0)


def _peer_select_body(x_ref, wqT_ref, g_ref, sk_ref, th_ref, p_ref, s2_ref, q_ref,
                      qT_ref, t_ref, work_ref, *, tm):
    xb = x_ref[...].astype(BF16)
    qT_ref[...] = _dot_nt(wqT_ref[...], xb)
    t_ref[...] = jnp.full(t_ref.shape, -jnp.inf, F32)
    rows8 = lax.broadcasted_iota(jnp.int32, (SUBLANES, tm), 0)

    def head(h, carry):
        qh = qT_ref[pl.ds(pl.multiple_of(h * D_KEY, D_KEY), D_KEY), :]
        ms = jnp.mean(qh * qh, axis=0, keepdims=True)
        qn = (qh * lax.rsqrt(ms + LN_EPS) * g_ref[...]).astype(BF16)
        s1 = _dot(sk_ref[h, 0], qn[0:128])
        s2 = _dot(sk_ref[h, 1], qn[128:256])
        work_ref[0:N_KEYS, :] = s1
        _extract_maxima(work_ref, t_ref, 0, N_KEYS)
        work_ref[0:N_KEYS, :] = s2
        _extract_maxima(work_ref, t_ref, 1, N_KEYS)
        t2 = t_ref[1, 0:T_ROWS, :]
        off = 0
        for a in range(N_EXTRACT):
            nb = N_EXTRACT // (a + 1)
            ng = -(-nb // SUBLANES)
            c = t_ref[0, a:a + 1, :] + t2[0:ng * SUBLANES]
            if nb < SUBLANES:
                c = jnp.where(rows8 < nb, c, -jnp.inf)
            work_ref[off:off + ng * SUBLANES, :] = c
            t_ref[2, off:off + ng * SUBLANES, :] = c
            off += ng * SUBLANES
        _extract_maxima(work_ref, t_ref, 3, N_CAND_ROWS)
        c16 = t_ref[3, PEER_TOPK - 1:PEER_TOPK, :]
        c17 = t_ref[3, PEER_TOPK:PEER_TOPK + 1, :]
        tau = 0.5 * (c16 + c17)
        t1max = t_ref[0, 0:1, :]
        t2max = t_ref[1, 0:1, :]
        cand = t_ref[2, 0:N_CAND_ROWS, :]
        z = jnp.sum(jnp.where(cand > tau, jnp.exp(cand - (t1max + t2max)), 0.0), axis=0, keepdims=True)
        th_ref[h] = tau - s1
        p_ref[h] = jnp.exp(s1 - t1max) / z
        s2_ref[h] = s2
        q_ref[h] = jnp.exp(s2 - t2max)
        return carry

    lax.fori_loop(0, PEER_HEADS, head, 0)


def _peer_select(x, w, tm):
    n = x.shape[0]
    sel = pl.BlockSpec((PEER_HEADS, N_KEYS, tm), lambda i: (0, 0, i))
    full = lambda arr: pl.BlockSpec(arr.shape, lambda i: (0,) * arr.ndim)
    ws = (w['wqT'], w['qg'], w['sk'])
    t_rows = max(T_ROWS, N_CAND_ROWS)
    return pl.pallas_call(
        functools.partial(_peer_select_body, tm=tm), grid=(n // tm,),
        in_specs=[pl.BlockSpec((tm, 1024), lambda i: (i, 0))] + [full(a) for a in ws],
        out_specs=(sel,) * 4,
        out_shape=(jax.ShapeDtypeStruct((PEER_HEADS, N_KEYS, n), F32),) * 4,
        scratch_shapes=[pltpu.VMEM((PEER_HEADS * D_KEY, tm), F32),
                        pltpu.VMEM((4, t_rows, tm), F32),
                        pltpu.VMEM((N_CAND_ROWS, tm), F32)],
        compiler_params=_cparams(("parallel",)), name="peer_select",
    )(x, *ws)


def _peer_dense_body(x_ref, th_ref, p_ref, s2_ref, q_ref, u_ref, vT_ref, g_ref, b_ref, o_ref,
                     xT_ref, acc_ref, h_ref, y_ref, *, te, tm, n_chunks, alpha):
    e = pl.program_id(1)
    keys_per_chunk = te // N_KEYS
    unit = 2 * N_KEYS

    @pl.when(e == 0)
    def _init():
        xT_ref[...] = x_ref[...].T.astype(BF16)
        acc_ref[...] = jnp.zeros(acc_ref.shape, F32)

    y_ref[1] = jnp.zeros(y_ref.shape[1:], BF16)

    def pair(ip, carry):
        base = pl.multiple_of(ip * unit, unit)
        prev = pl.multiple_of(jnp.maximum(ip - 1, 0) * unit, unit)
        slot = ip % 2
        row0 = e * keys_per_chunk + 2 * ip
        acc_ref[...] += _dot(vT_ref[:, pl.ds(prev, unit)], y_ref[1 - slot])
        h_ref[...] = _dot(u_ref[pl.ds(base, unit), :], xT_ref[...])
        th0 = th_ref[row0]
        th1 = th_ref[row0 + 1]
        pp0 = p_ref[row0]
        pp1 = p_ref[row0 + 1]
        for tg in range(tm // LANES):
            ls = slice(tg * LANES, (tg + 1) * LANES)
            w0 = None
            w1 = None
            for h in range(PEER_HEADS):
                s2t = s2_ref[h, :, ls]
                qt = q_ref[h, :, ls]
                t0 = jnp.where(s2t >= th0[h:h + 1, ls], qt, 0.0) * pp0[h:h + 1, ls]
                t1 = jnp.where(s2t >= th1[h:h + 1, ls], qt, 0.0) * pp1[h:h + 1, ls]
                w0 = t0 if w0 is None else w0 + t0
                w1 = t1 if w1 is None else w1 + t1
            for k, wgt in enumerate((w0, w1)):
                hh = h_ref[k * N_KEYS:(k + 1) * N_KEYS, ls]
                act = 0.5 * hh * (1.0 + lax.erf(hh * math.sqrt(0.5)))
                y_ref[slot, k * N_KEYS:(k + 1) * N_KEYS, ls] = (wgt * act).astype(BF16)
        return carry

    n_pairs = te // unit
    lax.fori_loop(0, n_pairs, pair, 0)
    acc_ref[...] += _dot(vT_ref[:, (n_pairs - 1) * unit:n_pairs * unit], y_ref[(n_pairs - 1) % 2])

    @pl.when(e == n_chunks - 1)
    def _finish():
        z = alpha * x_ref[...] + acc_ref[...].T
        o_ref[...] = _layer_norm(z, g_ref[...], b_ref[...])


def _peer_dense(x, sel, w, alpha, tm, te):
    n = x.shape[0]
    n_exp = w['u'].shape[0]
    n_chunks = n_exp // te
    tok = pl.BlockSpec((tm, 1024), lambda i, e: (i, 0))
    selspec = pl.BlockSpec((PEER_HEADS, N_KEYS, tm), lambda i, e: (0, 0, i))
    rowspec = pl.BlockSpec((N_KEYS, PEER_HEADS, tm), lambda i, e: (0, 0, i))
    full = lambda arr: pl.BlockSpec(arr.shape, lambda i, e: (0,) * arr.ndim)
    th, pp, s2, qq = sel
    sel = (th.transpose(1, 0, 2), pp.transpose(1, 0, 2), s2, qq)
    return pl.pallas_call(
        functools.partial(_peer_dense_body, te=te, tm=tm, n_chunks=n_chunks, alpha=alpha),
        grid=(n // tm, n_chunks),
        in_specs=[tok, rowspec, rowspec, selspec, selspec,
                  pl.BlockSpec((te, 1024), lambda i, e: (e, 0)),
                  pl.BlockSpec((1024, te), lambda i, e: (0, e)),
                  full(w['ln2_g']), full(w['ln2_b'])],
        out_specs=tok, out_shape=jax.ShapeDtypeStruct((n, 1024), F32),
        scratch_shapes=[pltpu.VMEM((1024, tm), BF16), pltpu.VMEM((1024, tm), F32),
                        pltpu.VMEM((2 * N_KEYS, tm), F32), pltpu.VMEM((2, 2 * N_KEYS, tm), BF16)],
        compiler_params=_cparams(("parallel", "arbitrary")), name="peer_dense",
    )(x, *sel, w['u'], w['vT'], w['ln2_g'], w['ln2_b'])


def _rope_tables(pos):
    half = ROPE_DIM // 2
    inv_freq = ROPE_THETA ** (-jnp.arange(0, ROPE_DIM, 2, dtype=F32) / ROPE_DIM)
    ang = pos.astype(F32)[:, None] * inv_freq[None, :]
    cos = jnp.cos(ang)
    sin = jnp.sin(ang)
    n = pos.shape[0]
    one = jnp.ones((n, 64 - ROPE_DIM), F32)
    zero8 = jnp.zeros((n, half), F32)
    zero = jnp.zeros((n, 64 - ROPE_DIM), F32)
    rc = jnp.concatenate([cos, cos, one], axis=1)
    rs1 = jnp.concatenate([-sin, zero8, zero], axis=1)
    rs2 = jnp.concatenate([zero8, sin, zero], axis=1)
    return tuple(jnp.concatenate([a, a], axis=1) for a in (rc, rs1, rs2))


def _layer_weights(l, w_in, b_forget, b_gate, w_branch_a, w_branch_b, w_out, ln1_g, ln1_b, ln2_g, ln2_b,
                   peer_wq, peer_q_g, peer_subkeys, peer_u, peer_v):
    wi = w_in[l]
    wfl = wi[:, 3072:3080]
    return {
        'wqk': wi[:, 0:1024].astype(BF16),
        'wrest': wi[:, 1024:3072].astype(BF16),
        'wg': wi[:, 3080:5128].astype(BF16),
        'wfl': jnp.pad(wfl, ((0, 0), (0, LANES - H_B))).astype(BF16),
        'wflT': jnp.pad(wfl.T, ((0, 16 - H_B), (0, 0))).astype(BF16),
        'bf': b_forget[l].reshape(1, H_B),
        'bfT': b_forget[l].reshape(H_B, 1),
        'bg': b_gate[l].reshape(1, 2048),
        'wba': w_branch_a[l].astype(BF16),
        'wbb': w_branch_b[l].astype(BF16),
        'wo': w_out[l].astype(BF16),
        'ln1_g': ln1_g[l].reshape(1, 1024), 'ln1_b': ln1_b[l].reshape(1, 1024),
        'ln2_g': ln2_g[l].reshape(1, 1024), 'ln2_b': ln2_b[l].reshape(1, 1024),
        'wqT': peer_wq[l].T.astype(BF16),
        'qg': peer_q_g[l].reshape(D_KEY, 1),
        'sk': peer_subkeys[l].astype(BF16),
        'u': peer_u[l].astype(BF16),
        'vT': peer_v[l].T.astype(BF16),
    }


def kernel(x_prompt, x_sample, cache_diff_k, cache_diff_v, cache_fox_k, cache_fox_v, cache_fox_logf, page_table, w_in, b_forget, b_gate, diff_lambda, diff_subln_g, w_branch_a, w_branch_b, w_out, ln1_g, ln1_b, ln2_g, ln2_b, peer_wq, peer_q_g, peer_subkeys, peer_u, peer_v):
    depth = w_in.shape[0]
    b, t, d = x_prompt.shape
    nb, ts, _ = x_sample.shape
    page = cache_diff_k.shape[2]
    past_len = page_table.shape[1] * page
    n_phys = cache_diff_k.shape[1]
    alpha = (2.0 * depth) ** 0.25
    assert ts == 4 and d == 1024 and page == 128

    xp = x_prompt.reshape(b * t, d)
    xs = x_sample.reshape(nb * ts, d)
    tabs_p = _rope_tables(jnp.arange(t, dtype=jnp.int32))
    tabs_s = _rope_tables(jnp.tile(past_len + jnp.arange(ts, dtype=jnp.int32), nb))

    ck = cache_diff_k.reshape(depth, n_phys, page * H_A, 2 * DH_A)
    cv = cache_diff_v.reshape(depth, n_phys, page * H_A, DV_A)
    fk_c = jnp.transpose(cache_fox_k, (0, 1, 3, 4, 2))
    fv_c = jnp.transpose(cache_fox_v, (0, 1, 3, 4, 2))
    lfT_pages = jnp.swapaxes(cache_fox_logf, 2, 3).reshape(depth * n_phys * H_B, page)
    suf, tot = _page_scan(lfT_pages)
    suf = suf.reshape(depth, n_phys, H_B, page)
    tot = tot.reshape(depth, n_phys, H_B, page)

    n_p = b * t
    tm_p = 256 if n_p % 256 == 0 else 128
    tm_e = 512 if n_p % 512 == 0 else 128
    tq = min(256, t)
    n_s = nb * ts
    rows_p = [[], [], [], [], []]
    rows_s = [[], [], [], [], []]
    for l in range(depth):
        w = _layer_weights(l, w_in, b_forget, b_gate, w_branch_a, w_branch_b, w_out, ln1_g, ln1_b,
                           ln2_g, ln2_b, peer_wq, peer_q_g, peer_subkeys, peer_u, peer_v)
        lam_init = 0.8 - 0.6 * math.exp(-0.3 * l)
        lam_p = diff_lambda[l]
        g_sub = diff_subln_g[l].reshape(1, DV_A)

        (dq, dk, dkb, dv, dvb, fq, fk, fkb, fv, fvb, lf, lfT, gates) = _inproj(xp, w, tabs_p, tm_p)
        cum, cumT = _cumsum(lf, lfT, b, t)
        cq = cum.reshape(b, t, 4, 2).transpose(0, 2, 1, 3)
        ckT = cumT.reshape(4, 2, b, t).transpose(2, 0, 1, 3)
        a_p = _attn_prompt('diff', dq, dkb, dvb, (lam_p, g_sub), b, t, lam_init, tq)
        f_p = _attn_prompt('fox', fq, fkb, fvb, (cq, ckT), b, t, lam_init, tq)
        x1 = _merge(xp, a_p, f_p, gates, w, alpha, tm_e)
        sel = _peer_select(x1, w, tm_e)
        xp = _peer_dense(x1, sel, w, alpha, tm_e, 1024)
        for i, r in enumerate((dk, dv, fk, fv, lf)):
            rows_p[i].append(r)

        (dq, dk, dkb, dv, dvb, fq, fk, fkb, fv, fvb, lf, lfT, gates) = _inproj(xs, w, tabs_s, n_s)
        qd, knd, vnd = _decode_operands('diff', dq, dkb, dvb, nb)
        a_s = _attn_decode('diff', l, page_table, qd, knd, vnd, (lam_p, g_sub), ck, cv, lam_init, 8)
        qf, knf, vnf = _decode_operands('fox', fq, fkb, fvb, nb)
        lnat = jnp.pad(lf.reshape(nb, ts, H_B), ((0, 0), (0, DEC_ROWS - ts), (0, LANES - H_B)))
        lrow = jnp.pad(lfT.reshape(H_B, nb, ts).transpose(1, 0, 2), ((0, 0), (0, 0), (0, LANES - ts)))
        f_s = _attn_decode('fox', l, page_table, qf, knf, vnf, (lnat, lrow, suf, tot), fk_c, fv_c, lam_init, 8)
        x1 = _merge(xs, a_s.reshape(n_s, 512), f_s.reshape(n_s, 512), gates, w, alpha, n_s)
        sel = _peer_select(x1, w, n_s)
        xs = _peer_dense(x1, sel, w, alpha, n_s, 1024)
        for i, r in enumerate((dk, dv, fk, fv, lf)):
            rows_s[i].append(r)

    def stack(rows, lead, tail):
        return jnp.stack(rows, axis=0).reshape((depth,) + lead + tail)

    lead_p, lead_s = (b, t), (nb, ts)
    tails = ((H_A, 2 * DH_A), (H_A, DV_A), (H_B, DH_B), (H_B, DH_B), (H_B,))
    outs_p = [stack(rows_p[i], lead_p, tails[i]) for i in range(5)]
    outs_s = [stack(rows_s[i], lead_s, tails[i]) for i in range(5)]
    return (xp.reshape(b, t, d), xs.reshape(nb, ts, d), *outs_p, *outs_s)
```

```python
import functools
import math

import jax
import jax.numpy as jnp
from jax import lax
from jax.experimental import pallas as pl
from jax.experimental.pallas import tpu as pltpu

F32 = jnp.float32
BF16 = jnp.bfloat16

H_A = 4
DH_A = 64
DV_A = 128
H_B = 8
DH_B = 64
ROPE_DIM = 16
ROPE_THETA = 500000.0
PEER_HEADS = 8
N_KEYS = 128
D_KEY = 256
PEER_TOPK = 16
LN_EPS = 1e-5
NEG_INF = -1e30
ATTN_SCALE = 0.125
LANES = 128
SUBLANES = 8
VMEM_LIMIT_BYTES = 56 * 1024 * 1024
N_CAND_ROWS = 152
N_EXTRACT = PEER_TOPK + 1
T_ROWS = 24
NEW_PAD = 16
DEC_ROWS = 16


def _cparams(sem):
    return pltpu.CompilerParams(dimension_semantics=sem, vmem_limit_bytes=VMEM_LIMIT_BYTES)


def _log_sigmoid(z):
    return jnp.minimum(z, 0.0) - jnp.log1p(jnp.exp(-jnp.abs(z)))


def _layer_norm(z, g, b):
    mu = jnp.mean(z, axis=-1, keepdims=True)
    zc = z - mu
    var = jnp.mean(zc * zc, axis=-1, keepdims=True)
    return zc * lax.rsqrt(var + LN_EPS) * g + b


def _dot(a, b):
    return jnp.dot(a, b, preferred_element_type=F32)


def _dot_nt(a, b):
    return lax.dot_general(a, b, (((1,), (1,)), ((), ())), preferred_element_type=F32)


def _inproj_body(*refs, n_alias):
    (x_ref, wqk_ref, wr_ref, wg_ref, wfl_ref, wflT_ref, bf_ref, bfT_ref, bg_ref,
     rc_ref, rs1_ref, rs2_ref) = refs[0:12]
    (dq_ref, dk_ref, dkb_ref, dv_ref, dvb_ref, fq_ref, fk_ref, fkb_ref, fv_ref, fvb_ref,
     lf_ref, lfT_ref, gate_ref) = refs[12 + n_alias:]
    xb = x_ref[...].astype(BF16)
    qk = _dot(xb, wqk_ref[...])
    rc = rc_ref[...]
    rs1 = rs1_ref[...]
    rs2 = rs2_ref[...]
    for blk in range(8):
        d = qk[:, blk * LANES:(blk + 1) * LANES]
        r = d * rc + pltpu.roll(d, LANES - 8, 1) * rs1 + pltpu.roll(d, 8, 1) * rs2
        if blk < 4:
            dq_ref[:, blk * LANES:(blk + 1) * LANES] = (r * ATTN_SCALE).astype(BF16)
        else:
            c = (blk - 4) * LANES
            dk_ref[0, :, c:c + LANES] = r
            dkb_ref[:, c:c + LANES] = r.astype(BF16)
    rest = _dot(xb, wr_ref[...])
    dv = rest[:, 0:512]
    dv_ref[0] = dv
    dvb_ref[...] = dv.astype(BF16)
    fq_ref[...] = (rest[:, 512:1024] * ATTN_SCALE).astype(BF16)
    fk = rest[:, 1024:1536]
    fk_ref[0] = fk
    fkb_ref[...] = fk.astype(BF16)
    fv = rest[:, 1536:2048]
    fv_ref[0] = fv
    fvb_ref[...] = fv.astype(BF16)
    gl = _dot(xb, wg_ref[...]) + bg_ref[...]
    gate_ref[...] = jax.nn.sigmoid(gl).astype(BF16)
    fl = _dot(xb, wfl_ref[...])[:, 0:H_B] + bf_ref[...]
    lf_ref[0] = _log_sigmoid(fl)
    flT = _dot_nt(wflT_ref[...], xb)[0:H_B, :] + bfT_ref[...]
    lfT_ref[...] = _log_sigmoid(flT)


ROW_OUTS = (1, 3, 6, 8, 10)


def _inproj(x, w, rope_tabs, tm, layer, depth, stacked):
    n = x.shape[0]
    rc, rs1, rs2 = rope_tabs
    nrep = rc.shape[0] // tm
    grid = (n // tm,)
    tok = lambda cols: pl.BlockSpec((tm, cols), lambda i: (i, 0))
    slab = lambda cols: pl.BlockSpec((1, tm, cols), lambda i: (layer, i, 0))
    full = lambda a: pl.BlockSpec(a.shape, lambda i: (0,) * a.ndim)
    tab = pl.BlockSpec((tm, LANES), lambda i: (i % nrep, 0))
    stk = lambda cols: jax.ShapeDtypeStruct((depth, n, cols), F32)
    out_shape = (
        jax.ShapeDtypeStruct((n, 512), BF16),
        stk(512),
        jax.ShapeDtypeStruct((n, 512), BF16),
        stk(512),
        jax.ShapeDtypeStruct((n, 512), BF16),
        jax.ShapeDtypeStruct((n, 512), BF16),
        stk(512),
        jax.ShapeDtypeStruct((n, 512), BF16),
        stk(512),
        jax.ShapeDtypeStruct((n, 512), BF16),
        stk(H_B),
        jax.ShapeDtypeStruct((H_B, n), F32),
        jax.ShapeDtypeStruct((n, 2048), BF16),
    )
    out_specs = (tok(512), slab(512), tok(512), slab(512), tok(512), tok(512), slab(512), tok(512),
                 slab(512), tok(512), slab(H_B), pl.BlockSpec((H_B, tm), lambda i: (0, i)), tok(2048))
    ins = (x, w['wqk'], w['wrest'], w['wg'], w['wfl'], w['wflT'], w['bf'], w['bfT'], w['bg'])
    in_specs = [tok(1024)] + [full(a) for a in ins[1:]] + [tab, tab, tab]
    stacked = tuple(stacked) if stacked is not None else ()
    in_specs += [pl.BlockSpec(memory_space=pl.ANY)] * len(stacked)
    aliases = {12 + k: ROW_OUTS[k] for k in range(len(stacked))}
    return pl.pallas_call(
        functools.partial(_inproj_body, n_alias=len(stacked)), grid=grid, in_specs=in_specs,
        out_specs=out_specs, out_shape=out_shape, input_output_aliases=aliases,
        compiler_params=_cparams(("parallel",)), name="inproj",
    )(*ins, rc, rs1, rs2, *stacked)


def _split3(a):
    hi = a.astype(BF16)
    r1 = a - hi.astype(F32)
    mid = r1.astype(BF16)
    lo = (r1 - mid.astype(F32)).astype(BF16)
    return hi, mid, lo


def _cumsum_body(lf_ref, lfT_ref, cum_ref, cumT_ref, *, t, cb):
    rows = lax.broadcasted_iota(jnp.int32, (cb, cb), 0)
    cols = lax.broadcasted_iota(jnp.int32, (cb, cb), 1)
    tri_l = jnp.where(rows >= cols, 1.0, 0.0).astype(BF16)
    tri_u = jnp.where(rows <= cols, 1.0, 0.0).astype(BF16)
    carry = jnp.zeros((1, H_B), F32)
    carry_t = jnp.zeros((H_B, 1), F32)
    for blk in range(t // cb):
        lb = lf_ref[0, blk * cb:(blk + 1) * cb, :]
        hi, mid, lo = _split3(lb)
        loc = _dot(tri_l, hi) + _dot(tri_l, mid) + _dot(tri_l, lo)
        cum_ref[blk * cb:(blk + 1) * cb, :] = loc + carry
        carry = carry + loc[cb - 1:cb, :]
        lbt = lfT_ref[:, blk * cb:(blk + 1) * cb]
        hi, mid, lo = _split3(lbt)
        loct = _dot(hi, tri_u) + _dot(mid, tri_u) + _dot(lo, tri_u)
        cumT_ref[:, blk * cb:(blk + 1) * cb] = loct + carry_t
        carry_t = carry_t + loct[:, cb - 1:cb]


def _cumsum(lf_stack, layer, lfT, b, t):
    cb = min(256, t)
    return pl.pallas_call(
        functools.partial(_cumsum_body, t=t, cb=cb), grid=(b,),
        in_specs=[pl.BlockSpec((1, t, H_B), lambda i: (layer, i, 0)), pl.BlockSpec((H_B, t), lambda i: (0, i))],
        out_specs=(pl.BlockSpec((t, H_B), lambda i: (i, 0)), pl.BlockSpec((H_B, t), lambda i: (0, i))),
        out_shape=(jax.ShapeDtypeStruct(lf_stack.shape[1:], F32), jax.ShapeDtypeStruct(lfT.shape, F32)),
        compiler_params=_cparams(("parallel",)), name="logf_cumsum",
    )(lf_stack, lfT)


def _diff_lambda(lp, lam_init):
    a = jnp.sum(lp[0:1, :] * lp[1:2, :], axis=1, keepdims=True)
    b = jnp.sum(lp[2:3, :] * lp[3:4, :], axis=1, keepdims=True)
    return jnp.exp(a) - jnp.exp(b) + lam_init


def _subln(a, g, lam_init):
    ms = jnp.mean(a * a, axis=-1, keepdims=True)
    return a * lax.rsqrt(ms + LN_EPS) * g * (1.0 - lam_init)


def _attn_prompt_body(*refs, mode, tq, lam_init):
    if mode == 'diff':
        q_ref, k_ref, v_ref, lam_ref, g_ref, o_ref, s_ref, mx_ref, ls_ref, acc_ref = refs
    else:
        q_ref, k_ref, v_ref, cq_ref, ckT_ref, o_ref, s_ref, mx_ref, ls_ref, acc_ref = refs
    tk = tq
    qi = pl.program_id(2)
    q = q_ref[...]
    lane = lax.broadcasted_iota(jnp.int32, (tq, LANES), 1)
    zero = jnp.zeros_like(q)
    q2 = jnp.concatenate([jnp.where(lane < 64, q, zero), jnp.where(lane >= 64, q, zero)], axis=0)
    if mode == 'fox':
        cq = cq_ref[0, 0]
        cq2 = jnp.concatenate([cq[:, 0:1], cq[:, 1:2]], axis=0)

    def scores(j):
        k = k_ref[0, pl.ds(pl.multiple_of(j * tk, tk), tk), :]
        s = _dot_nt(q2, k)
        if mode == 'fox':
            ck = ckT_ref[0, 0, :, pl.ds(pl.multiple_of(j * tk, tk), tk)]
            ck2 = jnp.concatenate([jnp.broadcast_to(ck[0:1, :], (tq, tk)),
                                   jnp.broadcast_to(ck[1:2, :], (tq, tk))], axis=0)
            s = s + (cq2 - ck2)
        return s

    def fold_max(s):
        m = mx_ref[...]
        for c0 in range(0, tk, LANES):
            m = jnp.maximum(m, s[:, c0:c0 + LANES])
        mx_ref[...] = m

    mx_ref[...] = jnp.full(mx_ref.shape, -jnp.inf, F32)

    def sweep1(j, carry):
        s = scores(j)
        s_ref[j] = s
        fold_max(s)
        return carry

    lax.fori_loop(0, qi, sweep1, 0)
    row = lax.broadcasted_iota(jnp.int32, (tq, tk), 0)
    col = lax.broadcasted_iota(jnp.int32, (tq, tk), 1)
    causal = jnp.concatenate([col <= row, col <= row], axis=0)
    s = jnp.where(causal, scores(qi), NEG_INF)
    s_ref[qi] = s
    fold_max(s)
    m = jnp.max(mx_ref[...], axis=1, keepdims=True)
    mx_ref[...] = jnp.broadcast_to(m, mx_ref.shape)
    ls_ref[...] = jnp.zeros(ls_ref.shape, F32)
    acc_ref[...] = jnp.zeros(acc_ref.shape, F32)

    def sweep2(j, carry):
        mb = mx_ref[...]
        p = jnp.exp(s_ref[j] - jnp.concatenate([mb] * (tk // LANES), axis=1))
        ls = ls_ref[...]
        for c0 in range(0, tk, LANES):
            ls = ls + p[:, c0:c0 + LANES]
        ls_ref[...] = ls
        v = v_ref[0, pl.ds(pl.multiple_of(j * tk, tk), tk), :]
        acc_ref[...] += _dot(p.astype(BF16), v)
        return carry

    lax.fori_loop(0, qi + 1, sweep2, 0)
    o = acc_ref[...] / jnp.sum(ls_ref[...], axis=1, keepdims=True)
    if mode == 'diff':
        lam = _diff_lambda(lam_ref[...], lam_init)
        a = o[0:tq] - lam * o[tq:2 * tq]
        o_ref[...] = _subln(a, g_ref[...], lam_init).astype(BF16)
    else:
        o_ref[...] = jnp.where(lane < 64, o[0:tq], o[tq:2 * tq]).astype(BF16)


def _attn_prompt(mode, q, k, v, extra, b, t, lam_init, tq):
    n = b * t
    nq = t // tq
    k3 = k.reshape(b, t, 512)
    v3 = v.reshape(b, t, 512)
    qspec = pl.BlockSpec((tq, LANES), lambda bi, h, qi: (bi * nq + qi, h))
    kspec = pl.BlockSpec((1, t, LANES), lambda bi, h, qi: (bi, 0, h))
    if mode == 'diff':
        lam_p, g = extra
        ex_specs = [pl.BlockSpec(lam_p.shape, lambda bi, h, qi: (0, 0)),
                    pl.BlockSpec(g.shape, lambda bi, h, qi: (0, 0))]
        ex = (lam_p, g)
    else:
        cq, ckT = extra
        ex_specs = [pl.BlockSpec((1, 1, tq, 2), lambda bi, h, qi: (bi, h, qi, 0)),
                    pl.BlockSpec((1, 1, 2, t), lambda bi, h, qi: (bi, h, 0, 0))]
        ex = (cq, ckT)
    scratch = [pltpu.VMEM((nq, 2 * tq, tq), F32), pltpu.VMEM((2 * tq, LANES), F32),
               pltpu.VMEM((2 * tq, LANES), F32), pltpu.VMEM((2 * tq, LANES), F32)]
    return pl.pallas_call(
        functools.partial(_attn_prompt_body, mode=mode, tq=tq, lam_init=lam_init),
        grid=(b, 4, nq), in_specs=[qspec, kspec, kspec] + ex_specs, out_specs=qspec,
        out_shape=jax.ShapeDtypeStruct((n, 512), BF16), scratch_shapes=scratch,
        compiler_params=_cparams(("parallel", "parallel", "arbitrary")), name="attn_prompt_" + mode,
    )(q, k3, v3, *ex)


def _attn_decode_body(*refs, mode, npg, ngroups, lam_init):
    refs = refs[1:]
    diff = mode == 'diff'
    nh = H_A if diff else H_B
    q_ref, kn_ref, vn_ref = refs[0:3]
    if diff:
        lam_ref, g_ref = refs[3:5]
        pos = 5
    else:
        lnat_ref, lrow_ref = refs[3:5]
        suf_refs = refs[5:5 + npg]
        tot_refs = refs[5 + npg:5 + 2 * npg]
        pos = 5 + 2 * npg
    k_refs = refs[pos:pos + npg]
    v_refs = refs[pos + npg:pos + 2 * npg]
    pos += 2 * npg
    o_ref, m_ref, l_ref, acc_ref = refs[pos:pos + 4]
    if not diff:
        carry_ref, cnc_ref, cnr_ref = refs[pos + 4:pos + 7]
    jj = pl.program_id(1)

    @pl.when(jj == 0)
    def _init():
        m_ref[...] = jnp.full(m_ref.shape, -jnp.inf, F32)
        l_ref[...] = jnp.zeros(l_ref.shape, F32)
        acc_ref[...] = jnp.zeros(acc_ref.shape, F32)
        if not diff:
            carry_ref[...] = jnp.zeros((SUBLANES, LANES), F32)
            ln = lnat_ref[0]
            rows = lax.broadcasted_iota(jnp.int32, (DEC_ROWS, LANES), 0)
            cn = jnp.zeros((DEC_ROWS, LANES), F32)
            for t in range(4):
                cn = cn + jnp.where(rows >= t, jnp.broadcast_to(ln[t:t + 1, :], (DEC_ROWS, LANES)), 0.0)
            for h in range(nh):
                cnc_ref[h] = jnp.broadcast_to(cn[:, h:h + 1], (DEC_ROWS, LANES))
            lr = lrow_ref[0]
            lanes = lax.broadcasted_iota(jnp.int32, (SUBLANES, LANES), 1)
            cr = jnp.zeros((SUBLANES, LANES), F32)
            for t in range(4):
                cr = cr + jnp.where(lanes >= t, jnp.broadcast_to(lr[:, t:t + 1], (SUBLANES, LANES)), 0.0)
            cnr_ref[...] = cr

    def update(s_heads, pv_fns):
        s = jnp.concatenate(s_heads, axis=0)
        m_old = m_ref[...]
        m_new = jnp.maximum(m_old, jnp.max(s, axis=1, keepdims=True))
        alpha = jnp.exp(m_old - m_new)
        p = jnp.exp(s - m_new)
        l_ref[...] = alpha * l_ref[...] + jnp.sum(p, axis=1, keepdims=True)
        pb = p.astype(BF16)
        pv = jnp.concatenate([fn(pb[h * DEC_ROWS:(h + 1) * DEC_ROWS]) for h, fn in enumerate(pv_fns)], axis=0)
        acc_ref[...] = alpha * acc_ref[...] + pv
        m_ref[...] = m_new

    if not diff:
        carry = carry_ref[...]
        bias = [None] * npg
        for p in reversed(range(npg)):
            bias[p] = suf_refs[p][0, 0] + carry
            carry = carry + tot_refs[p][0, 0]
        carry_ref[...] = carry

    s_heads, pv_fns = [], []
    for h in range(nh):
        if diff:
            kc = jnp.concatenate([r[0, 0, pl.ds(h, 128, stride=H_A), :].astype(BF16) for r in k_refs], axis=0)
            vc = jnp.concatenate([r[0, 0, pl.ds(h, 128, stride=H_A), :].astype(BF16) for r in v_refs], axis=0)
            s_heads.append(_dot_nt(q_ref[0, h], kc))
            pv_fns.append(lambda p, vc=vc: _dot(p, vc))
        else:
            kc = jnp.concatenate([r[0, 0, h].astype(BF16) for r in k_refs], axis=1)
            vc = jnp.concatenate([r[0, 0, h].astype(BF16) for r in v_refs], axis=1)
            bias_h = jnp.concatenate([b_[h:h + 1, :] for b_ in bias], axis=1)
            cncol = cnc_ref[h]
            s_heads.append(_dot(q_ref[0, h], kc) + (bias_h + jnp.concatenate([cncol] * npg, axis=1)))
            pv_fns.append(lambda p, vc=vc: _dot_nt(p, vc))
    update(s_heads, pv_fns)

    @pl.when(jj == ngroups - 1)
    def _finish():
        ncol = NEW_PAD if diff else LANES
        rows = lax.broadcasted_iota(jnp.int32, (DEC_ROWS, ncol), 0)
        tcol = lax.broadcasted_iota(jnp.int32, (DEC_ROWS, ncol), 1)
        t_of = jnp.minimum(rows % 4 if diff else rows, 3)
        visible = tcol <= t_of
        if diff:
            lam = _diff_lambda(lam_ref[...], lam_init)
            g = g_ref[...]
        sn_heads, pvn_fns = [], []
        for h in range(nh):
            vn = vn_ref[0, h]
            if diff:
                sn = _dot_nt(q_ref[0, h], kn_ref[0, h])
                pvn_fns.append(lambda p, vn=vn: _dot(p, vn))
            else:
                sn = _dot(q_ref[0, h], kn_ref[0, h])
                sn = sn + (cnc_ref[h] - cnr_ref[h:h + 1, :])
                pvn_fns.append(lambda p, vn=vn: _dot_nt(p, vn))
            sn_heads.append(jnp.where(visible, sn, NEG_INF))
        update(sn_heads, pvn_fns)
        o_all = acc_ref[...] / l_ref[...]
        for h in range(nh):
            o = o_all[h * DEC_ROWS:(h + 1) * DEC_ROWS]
            if diff:
                a = o[0:4] - lam * o[4:8]
                o_ref[0, :, h * DV_A:(h + 1) * DV_A] = _subln(a, g, lam_init)
            else:
                o_ref[0, :, h * DH_B:(h + 1) * DH_B] = o[0:4]


def _attn_decode(mode, layer, page_table, q, kn, vn, extra, cache_k, cache_v, lam_init, npg):
    nb, npages = page_table.shape
    ngroups = npages // npg
    diff = mode == 'diff'
    nh = H_A if diff else H_B
    dv = DV_A if diff else DH_B

    def page_idx(j, p):
        return (j if diff else (ngroups - 1 - j)) * npg + p

    def cache_spec(p):
        if diff:
            return pl.BlockSpec((1, 1, 4 * 128, 128), lambda b, j, pt: (layer, pt[b, page_idx(j, p)], 0, 0))
        return pl.BlockSpec((1, 1, H_B, DH_B, 128), lambda b, j, pt: (layer, pt[b, page_idx(j, p)], 0, 0, 0))

    small = lambda a: pl.BlockSpec((1,) + a.shape[1:], lambda b, j, pt: (b,) + (0,) * (a.ndim - 1))
    const = lambda a: pl.BlockSpec(a.shape, lambda b, j, pt: (0,) * a.ndim)
    ins = [q, kn, vn]
    in_specs = [small(q), small(kn), small(vn)]
    scratch = [pltpu.VMEM((nh * DEC_ROWS, 1), F32), pltpu.VMEM((nh * DEC_ROWS, 1), F32),
               pltpu.VMEM((nh * DEC_ROWS, dv), F32)]
    if diff:
        lam_p, g = extra
        ins += [lam_p, g]
        in_specs += [const(lam_p), const(g)]
    else:
        lnat, lrow, suf, tot = extra
        ins += [lnat, lrow] + [suf] * npg + [tot] * npg
        vspec = lambda p: pl.BlockSpec((1, 1, SUBLANES, LANES), lambda b, j, pt: (layer, pt[b, page_idx(j, p)], 0, 0))
        in_specs += [small(lnat), small(lrow)] + [vspec(p) for p in range(npg)] * 2
        scratch += [pltpu.VMEM((SUBLANES, LANES), F32), pltpu.VMEM((nh, DEC_ROWS, LANES), F32),
                    pltpu.VMEM((SUBLANES, LANES), F32)]
    ins += [cache_k] * npg + [cache_v] * npg
    in_specs += [cache_spec(p) for p in range(npg)] * 2
    grid_spec = pltpu.PrefetchScalarGridSpec(
        num_scalar_prefetch=1, grid=(nb, ngroups), in_specs=in_specs,
        out_specs=pl.BlockSpec((1, 4, 512), lambda b, j, pt: (b, 0, 0)), scratch_shapes=scratch)
    return pl.pallas_call(
        functools.partial(_attn_decode_body, mode=mode, npg=npg, ngroups=ngroups, lam_init=lam_init),
        grid_spec=grid_spec, out_shape=jax.ShapeDtypeStruct((nb, 4, 512), F32),
        compiler_params=_cparams(("parallel", "arbitrary")), name="attn_decode_" + mode,
    )(page_table, *ins)


def _decode_operands(mode, q, k, v, nb):
    if mode == 'diff':
        q4 = q.reshape(nb, 4, H_A, 128).transpose(0, 2, 1, 3)
        lane = jnp.arange(128) < DH_A
        qq = jnp.concatenate([jnp.where(lane, q4, 0), jnp.where(lane, 0, q4)], axis=2)
        qq = jnp.pad(qq, ((0, 0), (0, 0), (0, DEC_ROWS - 8), (0, 0)))
        rows = lambda a: jnp.pad(a.reshape(nb, 4, H_A, 128).transpose(0, 2, 1, 3),
                                 ((0, 0), (0, 0), (0, NEW_PAD - 4), (0, 0)))
        return qq, rows(k), rows(v)
    q4 = q.reshape(nb, 4, H_B, DH_B).transpose(0, 2, 1, 3)
    qq = jnp.pad(q4, ((0, 0), (0, 0), (0, DEC_ROWS - 4), (0, 0)))
    cols = lambda a: jnp.pad(a.reshape(nb, 4, H_B, DH_B).transpose(0, 2, 3, 1),
                             ((0, 0), (0, 0), (0, 0), (0, LANES - 4)))
    return qq, cols(k), cols(v)


def _page_scan_body(x_ref, suf_ref, tot_ref):
    rows = lax.broadcasted_iota(jnp.int32, (LANES, LANES), 0)
    cols = lax.broadcasted_iota(jnp.int32, (LANES, LANES), 1)
    upper = jnp.where(rows > cols, 1.0, 0.0).astype(BF16)
    ones = jnp.ones((LANES, LANES), BF16)
    hi, mid, lo = _split3(x_ref[...])
    suf_ref[...] = _dot(hi, upper) + _dot(mid, upper) + _dot(lo, upper)
    tot_ref[...] = _dot(hi, ones) + _dot(mid, ones) + _dot(lo, ones)


def _page_scan(lfT_pages):
    rows = lfT_pages.shape[0]
    tr = 2048 if rows % 2048 == 0 else rows
    spec = pl.BlockSpec((tr, LANES), lambda i: (i, 0))
    return pl.pallas_call(
        _page_scan_body, grid=(rows // tr,), in_specs=[spec], out_specs=(spec, spec),
        out_shape=(jax.ShapeDtypeStruct(lfT_pages.shape, F32),) * 2,
        compiler_params=_cparams(("parallel",)), name="logf_page_scan",
    )(lfT_pages)


def _merge_body(x_ref, a_ref, f_ref, gate_ref, wba_ref, wbb_ref, wo_ref, g_ref, b_ref, o_ref, *, alpha):
    ba = _dot(a_ref[...].astype(BF16), wba_ref[...])
    bb = _dot(f_ref[...].astype(BF16), wbb_ref[...])
    gates = gate_ref[...].astype(F32)
    merged = gates[:, 0:1024] * ba + gates[:, 1024:2048] * bb
    y = _dot(merged.astype(BF16), wo_ref[...])
    o_ref[...] = _layer_norm(alpha * x_ref[...] + y, g_ref[...], b_ref[...])


def _merge(x, a, f, gates, w, alpha, tm):
    n = x.shape[0]
    tok = lambda cols: pl.BlockSpec((tm, cols), lambda i: (i, 0))
    full = lambda arr: pl.BlockSpec(arr.shape, lambda i: (0,) * arr.ndim)
    ws = (w['wba'], w['wbb'], w['wo'], w['ln1_g'], w['ln1_b'])
    return pl.pallas_call(
        functools.partial(_merge_body, alpha=alpha), grid=(n // tm,),
        in_specs=[tok(1024), tok(512), tok(512), tok(2048)] + [full(a_) for a_ in ws],
        out_specs=tok(1024), out_shape=jax.ShapeDtypeStruct((n, 1024), F32),
        compiler_params=_cparams(("parallel",)), name="merge",
    )(x, a, f, gates, *ws)


def _extract_maxima(work_ref, out_ref, side, n_rows):
    def it(a, carry):
        w = work_ref[0:n_rows, :]
        mx = jnp.max(w, axis=0, keepdims=True)
        out_ref[side, pl.ds(a, 1), :] = mx
        work_ref[0:n_rows, :] = jnp.where(w == mx, -jnp.inf, w)
        return carry
    lax.fori_loop(0, N_EXTRACT, it, 0)


def _peer_select_body(x_ref, wqT_ref, g_ref, sk_ref, th_ref, p_ref, s2_ref, q_ref,
                      qT_ref, t_ref, work_ref, *, tm):
    xb = x_ref[...].astype(BF16)
    qT_ref[...] = _dot_nt(wqT_ref[...], xb)
    t_ref[...] = jnp.full(t_ref.shape, -jnp.inf, F32)
    rows8 = lax.broadcasted_iota(jnp.int32, (SUBLANES, tm), 0)

    def head(h, carry):
        qh = qT_ref[pl.ds(pl.multiple_of(h * D_KEY, D_KEY), D_KEY), :]
        ms = jnp.mean(qh * qh, axis=0, keepdims=True)
        qn = (qh * lax.rsqrt(ms + LN_EPS) * g_ref[...]).astype(BF16)
        s1 = _dot(sk_ref[h, 0], qn[0:128])
        s2 = _dot(sk_ref[h, 1], qn[128:256])
        work_ref[0:N_KEYS, :] = s1
        _extract_maxima(work_ref, t_ref, 0, N_KEYS)
        work_ref[0:N_KEYS, :] = s2
        _extract_maxima(work_ref, t_ref, 1, N_KEYS)
        t2 = t_ref[1, 0:T_ROWS, :]
        off = 0
        for a in range(N_EXTRACT):
            nb = N_EXTRACT // (a + 1)
            ng = -(-nb // SUBLANES)
            c = t_ref[0, a:a + 1, :] + t2[0:ng * SUBLANES]
            if nb < SUBLANES:
                c = jnp.where(rows8 < nb, c, -jnp.inf)
            work_ref[off:off + ng * SUBLANES, :] = c
            t_ref[2, off:off + ng * SUBLANES, :] = c
            off += ng * SUBLANES
        _extract_maxima(work_ref, t_ref, 3, N_CAND_ROWS)
        c16 = t_ref[3, PEER_TOPK - 1:PEER_TOPK, :]
        c17 = t_ref[3, PEER_TOPK:PEER_TOPK + 1, :]
        tau = 0.5 * (c16 + c17)
        t1max = t_ref[0, 0:1, :]
        t2max = t_ref[1, 0:1, :]
        cand = t_ref[2, 0:N_CAND_ROWS, :]
        z = jnp.sum(jnp.where(cand > tau, jnp.exp(cand - (t1max + t2max)), 0.0), axis=0, keepdims=True)
        th_ref[h] = tau - s1
        p_ref[h] = jnp.exp(s1 - t1max) / z
        s2_ref[h] = s2
        q_ref[h] = jnp.exp(s2 - t2max)
        return carry

    lax.fori_loop(0, PEER_HEADS, head, 0)


def _peer_select(x, w, tm):
    n = x.shape[0]
    sel = pl.BlockSpec((PEER_HEADS, N_KEYS, tm), lambda i: (0, 0, i))
    full = lambda arr: pl.BlockSpec(arr.shape, lambda i: (0,) * arr.ndim)
    ws = (w['wqT'], w['qg'], w['sk'])
    t_rows = max(T_ROWS, N_CAND_ROWS)
    return pl.pallas_call(
        functools.partial(_peer_select_body, tm=tm), grid=(n // tm,),
        in_specs=[pl.BlockSpec((tm, 1024), lambda i: (i, 0))] + [full(a) for a in ws],
        out_specs=(sel,) * 4,
        out_shape=(jax.ShapeDtypeStruct((PEER_HEADS, N_KEYS, n), F32),) * 4,
        scratch_shapes=[pltpu.VMEM((PEER_HEADS * D_KEY, tm), F32),
                        pltpu.VMEM((4, t_rows, tm), F32),
                        pltpu.VMEM((N_CAND_ROWS, tm), F32)],
        compiler_params=_cparams(("parallel",)), name="peer_select",
    )(x, *ws)


def _peer_dense_body(x_ref, th_ref, p_ref, s2_ref, q_ref, u_ref, vT_ref, g_ref, b_ref, o_ref,
                     xT_ref, acc_ref, h_ref, y_ref, *, te, tm, n_chunks, alpha):
    e = pl.program_id(1)
    keys_per_chunk = te // N_KEYS
    unit = 2 * N_KEYS

    @pl.when(e == 0)
    def _init():
        xT_ref[...] = x_ref[...].T.astype(BF16)
        acc_ref[...] = jnp.zeros(acc_ref.shape, F32)

    y_ref[1] = jnp.zeros(y_ref.shape[1:], BF16)

    def pair(ip, carry):
        base = pl.multiple_of(ip * unit, unit)
        prev = pl.multiple_of(jnp.maximum(ip - 1, 0) * unit, unit)
        slot = ip % 2
        row0 = e * keys_per_chunk + 2 * ip
        acc_ref[...] += _dot(vT_ref[:, pl.ds(prev, unit)], y_ref[1 - slot])
        h_ref[...] = _dot(u_ref[pl.ds(base, unit), :], xT_ref[...])
        th0 = th_ref[row0]
        th1 = th_ref[row0 + 1]
        pp0 = p_ref[row0]
        pp1 = p_ref[row0 + 1]
        for tg in range(tm // LANES):
            ls = slice(tg * LANES, (tg + 1) * LANES)
            w0 = None
            w1 = None
            for h in range(PEER_HEADS):
                s2t = s2_ref[h, :, ls]
                qt = q_ref[h, :, ls]
                t0 = jnp.where(s2t >= th0[h:h + 1, ls], qt, 0.0) * pp0[h:h + 1, ls]
                t1 = jnp.where(s2t >= th1[h:h + 1, ls], qt, 0.0) * pp1[h:h + 1, ls]
                w0 = t0 if w0 is None else w0 + t0
                w1 = t1 if w1 is None else w1 + t1
            for k, wgt in enumerate((w0, w1)):
                hh = h_ref[k * N_KEYS:(k + 1) * N_KEYS, ls]
                act = 0.5 * hh * (1.0 + lax.erf(hh * math.sqrt(0.5)))
                y_ref[slot, k * N_KEYS:(k + 1) * N_KEYS, ls] = (wgt * act).astype(BF16)
        return carry

    n_pairs = te // unit
    lax.fori_loop(0, n_pairs, pair, 0)
    acc_ref[...] += _dot(vT_ref[:, (n_pairs - 1) * unit:n_pairs * unit], y_ref[(n_pairs - 1) % 2])

    @pl.when(e == n_chunks - 1)
    def _finish():
        z = alpha * x_ref[...] + acc_ref[...].T
        o_ref[...] = _layer_norm(z, g_ref[...], b_ref[...])


def _peer_dense(x, sel, w, alpha, tm, te):
    n = x.shape[0]
    n_exp = w['u'].shape[0]
    n_chunks = n_exp // te
    tok = pl.BlockSpec((tm, 1024), lambda i, e: (i, 0))
    selspec = pl.BlockSpec((PEER_HEADS, N_KEYS, tm), lambda i, e: (0, 0, i))
    rowspec = pl.BlockSpec((N_KEYS, PEER_HEADS, tm), lambda i, e: (0, 0, i))
    full = lambda arr: pl.BlockSpec(arr.shape, lambda i, e: (0,) * arr.ndim)
    th, pp, s2, qq = sel
    sel = (th.transpose(1, 0, 2), pp.transpose(1, 0, 2), s2, qq)
    return pl.pallas_call(
        functools.partial(_peer_dense_body, te=te, tm=tm, n_chunks=n_chunks, alpha=alpha),
        grid=(n // tm, n_chunks),
        in_specs=[tok, rowspec, rowspec, selspec, selspec,
                  pl.BlockSpec((te, 1024), lambda i, e: (e, 0)),
                  pl.BlockSpec((1024, te), lambda i, e: (0, e)),
                  full(w['ln2_g']), full(w['ln2_b'])],
        out_specs=tok, out_shape=jax.ShapeDtypeStruct((n, 1024), F32),
        scratch_shapes=[pltpu.VMEM((1024, tm), BF16), pltpu.VMEM((1024, tm), F32),
                        pltpu.VMEM((2 * N_KEYS, tm), F32), pltpu.VMEM((2, 2 * N_KEYS, tm), BF16)],
        compiler_params=_cparams(("parallel", "arbitrary")), name="peer_dense",
    )(x, *sel, w['u'], w['vT'], w['ln2_g'], w['ln2_b'])


def _rope_tables(pos):
    half = ROPE_DIM // 2
    inv_freq = ROPE_THETA ** (-jnp.arange(0, ROPE_DIM, 2, dtype=F32) / ROPE_DIM)
    ang = pos.astype(F32)[:, None] * inv_freq[None, :]
    cos = jnp.cos(ang)
    sin = jnp.sin(ang)
    n = pos.shape[0]
    one = jnp.ones((n, 64 - ROPE_DIM), F32)
    zero8 = jnp.zeros((n, half), F32)
    zero = jnp.zeros((n, 64 - ROPE_DIM), F32)
    rc = jnp.concatenate([cos, cos, one], axis=1)
    rs1 = jnp.concatenate([-sin, zero8, zero], axis=1)
    rs2 = jnp.concatenate([zero8, sin, zero], axis=1)
    return tuple(jnp.concatenate([a, a], axis=1) for a in (rc, rs1, rs2))


def _layer_weights(l, w_in, b_forget, b_gate, w_branch_a, w_branch_b, w_out, ln1_g, ln1_b, ln2_g, ln2_b,
                   peer_wq, peer_q_g, peer_subkeys, peer_u, peer_v):
    wi = w_in[l]
    wfl = wi[:, 3072:3080]
    return {
        'wqk': wi[:, 0:1024].astype(BF16),
        'wrest': wi[:, 1024:3072].astype(BF16),
        'wg': wi[:, 3080:5128].astype(BF16),
        'wfl': jnp.pad(wfl, ((0, 0), (0, LANES - H_B))).astype(BF16),
        'wflT': jnp.pad(wfl.T, ((0, 16 - H_B), (0, 0))).astype(BF16),
        'bf': b_forget[l].reshape(1, H_B),
        'bfT': b_forget[l].reshape(H_B, 1),
        'bg': b_gate[l].reshape(1, 2048),
        'wba': w_branch_a[l].astype(BF16),
        'wbb': w_branch_b[l].astype(BF16),
        'wo': w_out[l].astype(BF16),
        'ln1_g': ln1_g[l].reshape(1, 1024), 'ln1_b': ln1_b[l].reshape(1, 1024),
        'ln2_g': ln2_g[l].reshape(1, 1024), 'ln2_b': ln2_b[l].reshape(1, 1024),
        'wqT': peer_wq[l].T.astype(BF16),
        'qg': peer_q_g[l].reshape(D_KEY, 1),
        'sk': peer_subkeys[l].astype(BF16),
        'u': peer_u[l].astype(BF16),
        'vT': peer_v[l].T.astype(BF16),
    }


def kernel(x_prompt, x_sample, cache_diff_k, cache_diff_v, cache_fox_k, cache_fox_v, cache_fox_logf, page_table, w_in, b_forget, b_gate, diff_lambda, diff_subln_g, w_branch_a, w_branch_b, w_out, ln1_g, ln1_b, ln2_g, ln2_b, peer_wq, peer_q_g, peer_subkeys, peer_u, peer_v):
    depth = w_in.shape[0]
    b, t, d = x_prompt.shape
    nb, ts, _ = x_sample.shape
    page = cache_diff_k.shape[2]
    past_len = page_table.shape[1] * page
    n_phys = cache_diff_k.shape[1]
    alpha = (2.0 * depth) ** 0.25
    assert ts == 4 and d == 1024 and page == 128

    xp = x_prompt.reshape(b * t, d)
    xs = x_sample.reshape(nb * ts, d)
    tabs_p = _rope_tables(jnp.arange(t, dtype=jnp.int32))
    tabs_s = _rope_tables(jnp.tile(past_len + jnp.arange(ts, dtype=jnp.int32), nb))

    ck = cache_diff_k.reshape(depth, n_phys, page * H_A, 2 * DH_A)
    cv = cache_diff_v.reshape(depth, n_phys, page * H_A, DV_A)
    fk_c = jnp.transpose(cache_fox_k, (0, 1, 3, 4, 2))
    fv_c = jnp.transpose(cache_fox_v, (0, 1, 3, 4, 2))
    lfT_pages = jnp.swapaxes(cache_fox_logf, 2, 3).reshape(depth * n_phys * H_B, page)
    suf, tot = _page_scan(lfT_pages)
    suf = suf.reshape(depth, n_phys, H_B, page)
    tot = tot.reshape(depth, n_phys, H_B, page)

    n_p = b * t
    tm_p = 256 if n_p % 256 == 0 else 128
    tm_e = 512 if n_p % 512 == 0 else 128
    tq = min(256, t)
    n_s = nb * ts
    npg = 16 if page_table.shape[1] % 16 == 0 else 8
    te = 2048
    stk_p = None
    stk_s = None
    for l in range(depth):
        w = _layer_weights(l, w_in, b_forget, b_gate, w_branch_a, w_branch_b, w_out, ln1_g, ln1_b,
                           ln2_g, ln2_b, peer_wq, peer_q_g, peer_subkeys, peer_u, peer_v)
        lam_init = 0.8 - 0.6 * math.exp(-0.3 * l)
        lam_p = diff_lambda[l]
        g_sub = diff_subln_g[l].reshape(1, DV_A)

        (dq, dk, dkb, dv, dvb, fq, fk, fkb, fv, fvb, lf, lfT, gates) = _inproj(
            xp, w, tabs_p, tm_p, l, depth, stk_p)
        stk_p = (dk, dv, fk, fv, lf)
        cum, cumT = _cumsum(lf, l, lfT, b, t)
        cq = cum.reshape(b, t, 4, 2).transpose(0, 2, 1, 3)
        ckT = cumT.reshape(4, 2, b, t).transpose(2, 0, 1, 3)
        a_p = _attn_prompt('diff', dq, dkb, dvb, (lam_p, g_sub), b, t, lam_init, tq)
        f_p = _attn_prompt('fox', fq, fkb, fvb, (cq, ckT), b, t, lam_init, tq)
        x1 = _merge(xp, a_p, f_p, gates, w, alpha, tm_e)
        sel = _peer_select(x1, w, tm_e)
        xp = _peer_dense(x1, sel, w, alpha, tm_e, te)

        (dq, dk, dkb, dv, dvb, fq, fk, fkb, fv, fvb, lf, lfT, gates) = _inproj(
            xs, w, tabs_s, n_s, l, depth, stk_s)
        stk_s = (dk, dv, fk, fv, lf)
        lf = lf[l]
        qd, knd, vnd = _decode_operands('diff', dq, dkb, dvb, nb)
        a_s = _attn_decode('diff', l, page_table, qd, knd, vnd, (lam_p, g_sub), ck, cv, lam_init, npg)
        qf, knf, vnf = _decode_operands('fox', fq, fkb, fvb, nb)
        lnat = jnp.pad(lf.reshape(nb, ts, H_B), ((0, 0), (0, DEC_ROWS - ts), (0, LANES - H_B)))
        lrow = jnp.pad(lfT.reshape(H_B, nb, ts).transpose(1, 0, 2), ((0, 0), (0, 0), (0, LANES - ts)))
        f_s = _attn_decode('fox', l, page_table, qf, knf, vnf, (lnat, lrow, suf, tot), fk_c, fv_c, lam_init, npg)
        x1 = _merge(xs, a_s.reshape(n_s, 512), f_s.reshape(n_s, 512), gates, w, alpha, n_s)
        sel = _peer_select(x1, w, n_s)
        xs = _peer_dense(x1, sel, w, alpha, n_s, te)

    lead_p, lead_s = (b, t), (nb, ts)
    tails = ((H_A, 2 * DH_A), (H_A, DV_A), (H_B, DH_B), (H_B, DH_B), (H_B,))
    outs_p = [stk_p[i].reshape((depth,) + lead_p + tails[i]) for i in range(5)]
    outs_s = [stk_s[i].reshape((depth,) + lead_s + tails[i]) for i in range(5)]
    return (xp.reshape(b, t, d), xs.reshape(nb, ts, d), *outs_p, *outs_s)
```
